```python
import jax, jax.numpy as jnp
from jax import lax
import numpy as np

D_MODEL = 1024
BATCH = 16
SEQ = 2048
DEPTH = 1

CTX_LEN = 256
GRID_W = 64
POS_THETA = 10000.0
RMS_EPS = 1e-6
N_MOD = 6
RG_WIDTH = 512
RG_BLOCKS = 8
RG_C = 8.0
CONV_W = 4
GLA_HEADS = 4
GLA_DK = 64
GLA_DV = 128
GLA_WIDTH = GLA_HEADS * GLA_DV
GLA_RANK = 16
GLA_GATE_NORM = 16.0
GLA_CHUNK = 64
MIX_WIDTH = RG_WIDTH + GLA_WIDTH
IN_SPLITS = (RG_WIDTH, RG_WIDTH, GLA_HEADS * GLA_DK, GLA_HEADS * GLA_DK, GLA_WIDTH, GLA_WIDTH, 2 * GLA_RANK)
IN_COLS = 2 * RG_WIDTH + 2 * GLA_HEADS * GLA_DK + 2 * GLA_WIDTH + 2 * GLA_RANK
N_KEYS = 128
N_EXPERTS = N_KEYS * N_KEYS
PEER_HEADS = 8
PEER_DQ = 256
PEER_TOPK = 16
PEER_TOKEN_BLOCK = 128

kernel_name = "hymba_rglru_gla_peer_dit_block"


def _rmsnorm(x, g):
    xf = x.astype(jnp.float32)
    y = xf * lax.rsqrt(jnp.mean(xf * xf, axis=-1, keepdims=True) + RMS_EPS)
    return (y * g.astype(jnp.float32)).astype(x.dtype)


def _modulate(x, shift, scale):
    return x * (1 + scale) + shift


def _sincos_2d(n_tokens, dim, dtype):
    rows = n_tokens // GRID_W
    r, col = jnp.meshgrid(jnp.arange(rows, dtype=jnp.float32), jnp.arange(GRID_W, dtype=jnp.float32), indexing="ij")
    quarter = dim // 4
    omega = POS_THETA ** (-jnp.arange(quarter, dtype=jnp.float32) / quarter)

    def axis_embed(p):
        ang = p.reshape(-1)[:, None] * omega[None, :]
        return jnp.concatenate([jnp.sin(ang), jnp.cos(ang)], axis=-1)

    return jnp.concatenate([axis_embed(r), axis_embed(col)], axis=-1).astype(dtype)


def _split_cols(h):
    bounds = np.cumsum(np.array(IN_SPLITS))[:-1].tolist()
    return jnp.split(h, bounds, axis=-1)


def _centred_dwconv(x, w, b):
    t = x.shape[1]
    left = CONV_W // 2
    right = CONV_W - 1 - left
    xp = jnp.pad(x, ((0, 0), (left, right), (0, 0)))
    out = xp[:, 0:t] * w[0]
    for j in range(1, CONV_W):
        out = out + xp[:, j:j + t] * w[j]
    return out + b


def _affine_combine(left, right):
    a_l, b_l = left
    a_r, b_r = right
    return a_l * a_r, a_r * b_l + b_r


def _linear_scan(a, b, h0, reverse):
    if reverse:
        a = jnp.flip(a, axis=1)
        b = jnp.flip(b, axis=1)
    b = b.at[:, 0].add(a[:, 0] * h0)
    _, h = lax.associative_scan(_affine_combine, (a, b), axis=1)
    if reverse:
        h = jnp.flip(h, axis=1)
    return h


def _rglru_direction(xc, w_a, b_a, w_x, b_x, lam, h0, reverse):
    bsz, t, width = xc.shape
    xf = xc.astype(jnp.float32)
    xb = xf.reshape(bsz, t, RG_BLOCKS, width // RG_BLOCKS)
    gate_r = jax.nn.sigmoid(jnp.einsum("btgi,gij->btgj", xb, w_a.astype(jnp.float32)).reshape(bsz, t, width) + b_a.astype(jnp.float32))
    gate_i = jax.nn.sigmoid(jnp.einsum("btgi,gij->btgj", xb, w_x.astype(jnp.float32)).reshape(bsz, t, width) + b_x.astype(jnp.float32))
    log_a = -RG_C * gate_r * jax.nn.softplus(-lam.astype(jnp.float32))
    a = jnp.exp(log_a)
    b = jnp.sqrt(-jnp.expm1(2.0 * log_a)) * gate_i * xf
    return _linear_scan(a, b, h0, reverse)


def _rglru_mixer(x_ctx, x_lat, conv_w, conv_b, w_a, b_a, w_x, b_x, lam):
    xc_ctx = _centred_dwconv(x_ctx, conv_w, conv_b)
    xc_lat = _centred_dwconv(x_lat, conv_w, conv_b)
    bsz = x_lat.shape[0]
    h0 = jnp.zeros((bsz, RG_WIDTH), jnp.float32)
    h_ctx_f = _rglru_direction(xc_ctx, w_a[0], b_a[0], w_x[0], b_x[0], lam[0], h0, False)
    h_lat_f = _rglru_direction(xc_lat, w_a[0], b_a[0], w_x[0], b_x[0], lam[0], h_ctx_f[:, -1], False)
    h_ctx_b = _rglru_direction(xc_ctx, w_a[1], b_a[1], w_x[1], b_x[1], lam[1], h0, True)
    h_lat_b = _rglru_direction(xc_lat, w_a[1], b_a[1], w_x[1], b_x[1], lam[1], h_ctx_b[:, 0], True)
    return h_ctx_f + h_ctx_b, h_lat_f + h_lat_b


def _to_heads(t, dh):
    bsz, n, _ = t.shape
    return t.reshape(bsz, n, GLA_HEADS, dh).transpose(0, 2, 1, 3).astype(jnp.float32)


def _gla_chunked(q, k, v, log_alpha, s0, reverse, with_output):
    if reverse:
        q, k, v, log_alpha = (jnp.flip(z, axis=2) for z in (q, k, v, log_alpha))
    bsz, heads, t, _ = q.shape
    dv = v.shape[-1]
    n_chunks = t // GLA_CHUNK

    def chunk(z):
        return z.reshape(bsz, heads, n_chunks, GLA_CHUNK, z.shape[-1])

    q, k, v, log_alpha = chunk(q), chunk(k), chunk(v), chunk(log_alpha)
    b = jnp.cumsum(log_alpha, axis=3)
    b_last = b[:, :, :, -1:, :]
    ds = jnp.einsum("bhncd,bhncv->bhndv", k * jnp.exp(b_last - b), v)
    gamma = jnp.broadcast_to(jnp.exp(b_last[:, :, :, 0, :])[..., None], ds.shape)
    ds = ds.at[:, :, 0].add(gamma[:, :, 0] * s0)
    _, s_out = lax.associative_scan(_affine_combine, (gamma, ds), axis=2)
    s_end = s_out[:, :, -1]
    if not with_output:
        return None, s_end
    s_in = jnp.concatenate([s0[:, :, None], s_out[:, :, :-1]], axis=2)
    q_dec = q * jnp.exp(b)
    k_inv = k * jnp.exp(-b)
    lower = jnp.tril(jnp.ones((GLA_CHUNK, GLA_CHUNK), dtype=bool))
    scores = jnp.where(lower, jnp.einsum("bhnid,bhnjd->bhnij", q_dec, k_inv), 0.0)
    o = jnp.einsum("bhnij,bhnjv->bhniv", scores, v) + jnp.einsum("bhnid,bhndv->bhniv", q_dec, s_in)
    o = o.reshape(bsz, heads, t, dv)
    if reverse:
        o = jnp.flip(o, axis=2)
    return o, s_end


def _log_gate(lr, w_g, b_g, direction):
    z = lr[..., direction * GLA_RANK:(direction + 1) * GLA_RANK].astype(jnp.float32) @ w_g.astype(jnp.float32) + b_g.astype(jnp.float32)
    return _to_heads(jax.nn.log_sigmoid(z) / GLA_GATE_NORM, GLA_DK)


def _gla_mixer(q_c, k_c, v_c, lr_c, q_l, k_l, v_l, lr_l, w_g, b_g, with_ctx_out):
    scale = GLA_DK ** -0.5
    qc, kc, vc = _to_heads(q_c, GLA_DK) * scale, _to_heads(k_c, GLA_DK), _to_heads(v_c, GLA_DV)
    ql, kl, vl = _to_heads(q_l, GLA_DK) * scale, _to_heads(k_l, GLA_DK), _to_heads(v_l, GLA_DV)
    bsz = ql.shape[0]
    s0 = jnp.zeros((bsz, GLA_HEADS, GLA_DK, GLA_DV), jnp.float32)
    outs_c, outs_l = [], []
    for d, reverse in enumerate((False, True)):
        o_c, s_ctx = _gla_chunked(qc, kc, vc, _log_gate(lr_c, w_g[d], b_g[d], d), s0, reverse, with_ctx_out)
        o_l, _ = _gla_chunked(ql, kl, vl, _log_gate(lr_l, w_g[d], b_g[d], d), s_ctx, reverse, True)
        outs_c.append(o_c)
        outs_l.append(o_l)
    o_ctx = outs_c[0] + outs_c[1] if with_ctx_out else None
    return o_ctx, outs_l[0] + outs_l[1]


def _merge_groups(y_rg, gate_rg, o_gla, r_gla, norm_g, w_out):
    dtype = gate_rg.dtype
    bsz, t, _ = gate_rg.shape
    rg = y_rg.astype(dtype) * jax.nn.gelu(gate_rg, approximate=False)
    gla = _rmsnorm(o_gla, norm_g).transpose(0, 2, 1, 3).reshape(bsz, t, GLA_WIDTH).astype(dtype)
    gla = gla * jax.nn.silu(r_gla)
    return jnp.concatenate([rg, gla], axis=-1) @ w_out


def _peer(u, w_q, keys, u_tab, v_tab):
    bsz, t, dim = u.shape
    q = (u @ w_q).reshape(bsz, t, PEER_HEADS, 2, PEER_DQ // 2).astype(jnp.float32)
    s = jnp.einsum("bthpd,hpkd->bthpk", q, keys.astype(jnp.float32))
    s1, i1 = lax.top_k(s[..., 0, :], PEER_TOPK)
    s2, i2 = lax.top_k(s[..., 1, :], PEER_TOPK)
    n_cand = PEER_TOPK * PEER_TOPK
    cand_s = (s1[..., :, None] + s2[..., None, :]).reshape(bsz, t, PEER_HEADS, n_cand)
    cand_e = (i1[..., :, None] * N_KEYS + i2[..., None, :]).reshape(bsz, t, PEER_HEADS, n_cand)
    top_s, pos = lax.top_k(cand_s, PEER_TOPK)
    expert = jnp.take_along_axis(cand_e, pos, axis=-1)
    weight = jax.nn.softmax(top_s, axis=-1).astype(u.dtype)
    n_blocks = (bsz * t) // PEER_TOKEN_BLOCK
    n_sel = PEER_HEADS * PEER_TOPK
    u_blk = u.reshape(n_blocks, PEER_TOKEN_BLOCK, dim)
    e_blk = expert.reshape(n_blocks, PEER_TOKEN_BLOCK, n_sel)
    w_blk = weight.reshape(n_blocks, PEER_TOKEN_BLOCK, n_sel)

    def block(args):
        xb, eb, wb = args
        act = jax.nn.gelu(jnp.einsum("td,tkd->tk", xb, u_tab[eb]), approximate=False) * wb
        return jnp.einsum("tk,tkd->td", act, v_tab[eb])

    y = lax.map(block, (u_blk, e_blk, w_blk))
    return y.reshape(bsz, t, dim)


def setup_inputs(seed: int = 0) -> dict:
    key = jax.random.key(seed)
    ks = jax.random.split(key, 25)
    f32 = jnp.float32
    d = D_MODEL

    def nrm(k, shape, s):
        return s * jax.random.normal(k, shape, f32)

    a0 = jax.random.uniform(ks[14], (DEPTH, 2, RG_WIDTH), f32, 0.9, 0.999)
    sig = a0 ** (1.0 / RG_C)
    rg_lambda = jnp.log(sig) - jnp.log1p(-sig)
    blk = RG_WIDTH // RG_BLOCKS
    return {
        "x": nrm(ks[0], (BATCH, SEQ, d), 1.0),
        "c": nrm(ks[1], (BATCH, d), 1.0),
        "ctx": nrm(ks[2], (BATCH, CTX_LEN, d), 1.0),
        "c_ctx": nrm(ks[3], (d,), 1.0),
        "ada_w": nrm(ks[4], (DEPTH, d, N_MOD * d), 0.3 * d ** -0.5),
        "ada_b": nrm(ks[5], (DEPTH, N_MOD * d), 0.02),
        "norm1_g": 1.0 + nrm(ks[6], (DEPTH, d), 0.05),
        "w_in": nrm(ks[7], (DEPTH, d, IN_COLS), d ** -0.5),
        "conv_w": nrm(ks[8], (DEPTH, CONV_W, RG_WIDTH), 0.5),
        "conv_b": nrm(ks[9], (DEPTH, RG_WIDTH), 0.02),
        "rg_w_a": nrm(ks[10], (DEPTH, 2, RG_BLOCKS, blk, blk), blk ** -0.5),
        "rg_b_a": nrm(ks[11], (DEPTH, 2, RG_WIDTH), 0.1),
        "rg_w_x": nrm(ks[12], (DEPTH, 2, RG_BLOCKS, blk, blk), blk ** -0.5),
        "rg_b_x": nrm(ks[13], (DEPTH, 2, RG_WIDTH), 0.1),
        "rg_lambda": rg_lambda,
        "gla_w_g": nrm(ks[15], (DEPTH, 2, GLA_RANK, GLA_HEADS * GLA_DK), GLA_RANK ** -0.5),
        "gla_b_g": 2.0 + nrm(ks[16], (DEPTH, 2, GLA_HEADS * GLA_DK), 0.5),
        "gla_norm_g": 1.0 + nrm(ks[17], (DEPTH, GLA_DV), 0.05),
        "w_out": nrm(ks[18], (DEPTH, MIX_WIDTH, d), MIX_WIDTH ** -0.5),
        "norm2_g": 1.0 + nrm(ks[19], (DEPTH, d), 0.05),
        "peer_w_q": nrm(ks[20], (DEPTH, d, PEER_HEADS * PEER_DQ), d ** -0.5),
        "peer_keys": nrm(ks[21], (DEPTH, PEER_HEADS, 2, N_KEYS, PEER_DQ // 2), (PEER_DQ // 2) ** -0.5),
        "peer_u": nrm(ks[22], (DEPTH, N_EXPERTS, d), d ** -0.5),
        "peer_v": nrm(ks[23], (DEPTH, N_EXPERTS, d), 1.0),
        "final_norm_g": 1.0 + nrm(ks[24], (d,), 0.05),
    }


def reference(x, c, ctx, c_ctx, ada_w, ada_b, norm1_g, w_in, conv_w, conv_b, rg_w_a, rg_b_a, rg_w_x, rg_b_x, rg_lambda, gla_w_g, gla_b_g, gla_norm_g, w_out, norm2_g, peer_w_q, peer_keys, peer_u, peer_v, final_norm_g):
    bsz, n_lat, dim = x.shape
    x = x + _sincos_2d(n_lat, dim, x.dtype)[None]
    for l in range(DEPTH):
        update_ctx = l < DEPTH - 1
        mod_l = (jax.nn.silu(c) @ ada_w[l] + ada_b[l]).reshape(bsz, N_MOD, 1, dim)
        mod_c = (jax.nn.silu(c_ctx) @ ada_w[l] + ada_b[l]).reshape(N_MOD, 1, 1, dim)
        h_l = _modulate(_rmsnorm(x, norm1_g[l]), mod_l[:, 0], mod_l[:, 1]) @ w_in[l]
        h_c = _modulate(_rmsnorm(ctx, norm1_g[l]), mod_c[0], mod_c[1]) @ w_in[l]
        rx_l, rgate_l, q_l, k_l, v_l, r_l, lr_l = _split_cols(h_l)
        rx_c, rgate_c, q_c, k_c, v_c, r_c, lr_c = _split_cols(h_c)
        y_c, y_l = _rglru_mixer(rx_c, rx_l, conv_w[l], conv_b[l], rg_w_a[l], rg_b_a[l], rg_w_x[l], rg_b_x[l], rg_lambda[l])
        o_c, o_l = _gla_mixer(q_c, k_c, v_c, lr_c, q_l, k_l, v_l, lr_l, gla_w_g[l], gla_b_g[l], update_ctx)
        x = x + mod_l[:, 2] * _merge_groups(y_l, rgate_l, o_l, r_l, gla_norm_g[l], w_out[l])
        u_l = _modulate(_rmsnorm(x, norm2_g[l]), mod_l[:, 3], mod_l[:, 4])
        x = x + mod_l[:, 5] * _peer(u_l, peer_w_q[l], peer_keys[l], peer_u[l], peer_v[l])
        if update_ctx:
            ctx = ctx + mod_c[2] * _merge_groups(y_c, rgate_c, o_c, r_c, gla_norm_g[l], w_out[l])
            u_c = _modulate(_rmsnorm(ctx, norm2_g[l]), mod_c[3], mod_c[4])
            ctx = ctx + mod_c[5] * _peer(u_c, peer_w_q[l], peer_keys[l], peer_u[l], peer_v[l])
    return _rmsnorm(x, final_norm_g)
```

```python
import functools
import math

import jax
import jax.numpy as jnp
from jax import lax
from jax.experimental import pallas as pl
from jax.experimental.pallas import tpu as pltpu

F32 = jnp.float32
BF16 = jnp.bfloat16

GRID_W = 64
POS_THETA = 10000.0
RMS_EPS = 1e-6
N_MOD = 6
RG_WIDTH = 512
RG_BLOCKS = 8
RG_C = 8.0
CONV_W = 4
GLA_HEADS = 4
GLA_DK = 64
GLA_DV = 128
GLA_WIDTH = GLA_HEADS * GLA_DV
GLA_RANK = 16
GLA_GATE_NORM = 16.0
GLA_CHUNK = 64
QK_WIDTH = GLA_HEADS * GLA_DK
N_KEYS = 128
PEER_HEADS = 8
PEER_DQ = 256
PEER_TOPK = 16
N_SEL = PEER_HEADS * PEER_TOPK

LANES = 128
SUBLANES = 8
VMEM_LIMIT = 56 * 1024 * 1024

COL_RX, COL_RGATE, COL_Q, COL_K, COL_V, COL_R, COL_LR = 0, 512, 1024, 1280, 1536, 2048, 2560
IN_COLS_PAD = 2688

SCAN_BLOCK = 128
HALO = 8
ROW_TILE = 512
SEL_TILE = 256
GATHER_TOKENS = 8


def _cparams(sem, vmem=VMEM_LIMIT):
    return pltpu.CompilerParams(dimension_semantics=sem, vmem_limit_bytes=vmem)


def _sincos_2d(n_tokens, dim):
    rows = n_tokens // GRID_W
    r, col = jnp.meshgrid(jnp.arange(rows, dtype=F32), jnp.arange(GRID_W, dtype=F32), indexing="ij")
    quarter = dim // 4
    omega = POS_THETA ** (-jnp.arange(quarter, dtype=F32) / quarter)

    def axis_embed(p):
        ang = p.reshape(-1)[:, None] * omega[None, :]
        return jnp.concatenate([jnp.sin(ang), jnp.cos(ang)], axis=-1)

    return jnp.concatenate([axis_embed(r), axis_embed(col)], axis=-1)


def _gelu(x):
    return 0.5 * x * (1.0 + lax.erf(x * (1.0 / math.sqrt(2.0))))


def _rms(x, g):
    ms = jnp.mean(x * x, axis=-1, keepdims=True)
    return x * lax.rsqrt(ms + RMS_EPS) * g


def _adaln_kernel(c_ref, w_ref, b_ref, o_ref):
    a = jax.nn.silu(c_ref[...]).astype(BF16)
    o_ref[...] = jnp.dot(a, w_ref[...].astype(BF16), preferred_element_type=F32) + b_ref[...]


def _adaln(cc, w, b):
    rows, d = cc.shape
    n = w.shape[1]
    tn = 1536
    return pl.pallas_call(
        _adaln_kernel,
        grid=(n // tn,),
        in_specs=[pl.BlockSpec((rows, d), lambda j: (0, 0)),
                  pl.BlockSpec((d, tn), lambda j: (0, j)),
                  pl.BlockSpec((1, tn), lambda j: (0, j))],
        out_specs=pl.BlockSpec((rows, tn), lambda j: (0, j)),
        out_shape=jax.ShapeDtypeStruct((rows, n), F32),
        compiler_params=_cparams(("arbitrary",)),
        name="adaln",
    )(cc, w, b)


def _inproj_kernel(*refs, has_pos):
    if has_pos:
        x_ref, pos_ref, shift_ref, scale_ref, g_ref, w_ref, o_ref = refs
        x = x_ref[...] + pos_ref[...]
    else:
        x_ref, shift_ref, scale_ref, g_ref, w_ref, o_ref = refs
        x = x_ref[...]
    y = _rms(x, g_ref[...]) * (1.0 + scale_ref[0]) + shift_ref[0]
    o_ref[...] = jnp.dot(y.astype(BF16), w_ref[...], preferred_element_type=F32)


def _inproj(x2, pos, shift, scale, g, w, seq_len):
    n, d = x2.shape
    cols = w.shape[1]
    tm = min(ROW_TILE, seq_len)
    per_seq = seq_len // tm
    shared = shift.shape[0] == 1
    mod_map = (lambda i: (0, 0, 0)) if shared else (lambda i: (i // per_seq, 0, 0))
    in_specs = [pl.BlockSpec((tm, d), lambda i: (i, 0))]
    args = [x2]
    if pos is not None:
        in_specs.append(pl.BlockSpec((tm, d), lambda i: (i % per_seq, 0)))
        args.append(pos)
    in_specs += [pl.BlockSpec((1, 1, d), mod_map), pl.BlockSpec((1, 1, d), mod_map),
                 pl.BlockSpec((1, d), lambda i: (0, 0)), pl.BlockSpec((d, cols), lambda i: (0, 0))]
    args += [shift, scale, g, w]
    return pl.pallas_call(
        functools.partial(_inproj_kernel, has_pos=pos is not None),
        grid=(n // tm,),
        in_specs=in_specs,
        out_specs=pl.BlockSpec((tm, cols), lambda i: (i, 0)),
        out_shape=jax.ShapeDtypeStruct((n, cols), F32),
        compiler_params=_cparams(("arbitrary",)),
        name="inproj",
    )(*args)


def _rglru_kernel(rxl_ref, rxc_ref, cw_ref, cb_ref, wg_ref, bg_ref, lam_ref, y_ref, xpl_ref, xpc_ref, *, t_lat, t_ctx):
    tb = SCAN_BLOCK
    w = RG_WIDTH
    zeros_halo = jnp.zeros((HALO, w), F32)
    for xp_ref, src_ref, t in ((xpl_ref, rxl_ref, t_lat), (xpc_ref, rxc_ref, t_ctx)):
        xp_ref[0:HALO, :] = zeros_halo
        xp_ref[HALO:HALO + t, :] = src_ref[...]
        xp_ref[HALO + t:HALO + t + HALO, :] = zeros_halo

    cw = cw_ref[...]
    cb = cb_ref[...]
    row = lax.broadcasted_iota(jnp.int32, (tb, w), 0)

    def block(xp_ref, r0, d, h_in):
        ext = xp_ref[pl.ds(r0, tb + 2 * HALO), :]
        xc = (ext[HALO - 2:HALO - 2 + tb] * cw[0:1] + ext[HALO - 1:HALO - 1 + tb] * cw[1:2]
              + ext[HALO:HALO + tb] * cw[2:3] + ext[HALO + 1:HALO + 1 + tb] * cw[3:4] + cb)
        g = jnp.dot(xc.astype(BF16), wg_ref[:, 2 * w * d:2 * w * (d + 1)], preferred_element_type=F32)
        g = g + bg_ref[:, 2 * w * d:2 * w * (d + 1)]
        gate_r = jax.nn.sigmoid(g[:, :w])
        gate_i = jax.nn.sigmoid(g[:, w:])
        log_a = (-RG_C * jax.nn.softplus(-lam_ref[d:d + 1, :])) * gate_r
        a = jnp.exp(log_a)
        th = jnp.tanh(log_a)
        bv = jnp.sqrt(-2.0 * th / (1.0 - th)) * gate_i * xc
        s = 1
        while s < tb:
            if d == 0:
                valid = row >= s
                a_p = jnp.where(valid, pltpu.roll(a, s, 0), 1.0)
                b_p = jnp.where(valid, pltpu.roll(bv, s, 0), 0.0)
            else:
                valid = row < tb - s
                a_p = jnp.where(valid, pltpu.roll(a, tb - s, 0), 1.0)
                b_p = jnp.where(valid, pltpu.roll(bv, tb - s, 0), 0.0)
            bv = a * b_p + bv
            a = a * a_p
            s *= 2
        h = a * h_in + bv
        h_out = h[tb - 1:tb] if d == 0 else h[0:1]
        return h, h_out

    n_lat = t_lat // tb
    n_ctx = t_ctx // tb
    h0 = jnp.zeros((1, w), F32)

    def ctx_f(i, h):
        return block(xpc_ref, pl.multiple_of(i * tb, tb), 0, h)[1]

    def lat_f(i, h):
        r0 = pl.multiple_of(i * tb, tb)
        hb, h_out = block(xpl_ref, r0, 0, h)
        y_ref[pl.ds(r0, tb), :] = hb
        return h_out

    h = lax.fori_loop(0, n_ctx, ctx_f, h0)
    lax.fori_loop(0, n_lat, lat_f, h)

    def ctx_b(i, h):
        return block(xpc_ref, pl.multiple_of((n_ctx - 1 - i) * tb, tb), 1, h)[1]

    def lat_b(i, h):
        r0 = pl.multiple_of((n_lat - 1 - i) * tb, tb)
        hb, h_out = block(xpl_ref, r0, 1, h)
        y_ref[pl.ds(r0, tb), :] = y_ref[pl.ds(r0, tb), :] + hb
        return h_out

    h = lax.fori_loop(0, n_ctx, ctx_b, h0)
    lax.fori_loop(0, n_lat, lat_b, h)


def _rglru(h_l, h_c, conv_w, conv_b, wg, bg, lam, batch, t_lat, t_ctx):
    w = RG_WIDTH
    kern = functools.partial(_rglru_kernel, t_lat=t_lat, t_ctx=t_ctx)
    full = lambda shape: pl.BlockSpec(shape, lambda b: tuple(0 for _ in shape))
    return pl.pallas_call(
        kern,
        grid=(batch,),
        in_specs=[pl.BlockSpec((t_lat, w), lambda b: (b, COL_RX // w)),
                  pl.BlockSpec((t_ctx, w), lambda b: (b, COL_RX // w)),
                  full((CONV_W, w)), full((1, w)), full((w, 4 * w)), full((1, 4 * w)), full((2, w))],
        out_specs=pl.BlockSpec((t_lat, w), lambda b: (b, 0)),
        out_shape=jax.ShapeDtypeStruct((batch * t_lat, w), F32),
        scratch_shapes=[pltpu.VMEM((t_lat + 2 * HALO, w), F32), pltpu.VMEM((t_ctx + 2 * HALO, w), F32)],
        compiler_params=_cparams(("arbitrary",)),
        name="rglru",
    )(h_l, h_c, conv_w, conv_b, wg, bg, lam)


def _gla_kernel(ql_ref, kl_ref, vl_ref, lrl_ref, kc_ref, vc_ref, lrc_ref, wg_ref, bg_ref, o_ref, *, t_lat, t_ctx):
    c = GLA_CHUNK
    qk = QK_WIDTH
    scale = GLA_DK ** -0.5
    row = lax.broadcasted_iota(jnp.int32, (c, qk), 0)
    ri = lax.broadcasted_iota(jnp.int32, (c, c), 0)
    ci = lax.broadcasted_iota(jnp.int32, (c, c), 1)

    def decays(lr_ref, r0, d):
        z = jnp.dot(lr_ref[pl.ds(r0, c), :].astype(BF16), wg_ref[d].astype(BF16), preferred_element_type=F32) + bg_ref[d]
        b = jax.nn.log_sigmoid(z) * (1.0 / GLA_GATE_NORM)
        s = 1
        while s < c:
            if d == 0:
                b = b + jnp.where(row >= s, pltpu.roll(b, s, 0), 0.0)
            else:
                b = b + jnp.where(row < c - s, pltpu.roll(b, c - s, 0), 0.0)
            s *= 2
        b_last = b[c - 1:c] if d == 0 else b[0:1]
        return b, b_last

    def state_update(st, k, v, b, b_last):
        k_dec = (k * jnp.exp(b_last - b)).astype(BF16)
        vb = v.astype(BF16)
        parts = []
        for h in range(GLA_HEADS):
            parts.append(lax.dot_general(vb[:, GLA_DV * h:GLA_DV * (h + 1)], k_dec[:, GLA_DK * h:GLA_DK * (h + 1)],
                                         (((0,), (0,)), ((), ())), preferred_element_type=F32))
        return st * jnp.exp(b_last) + jnp.concatenate(parts, axis=1)

    def ctx_chunk(r0, d, st):
        b, b_last = decays(lrc_ref, r0, d)
        return state_update(st, kc_ref[pl.ds(r0, c), :], vc_ref[pl.ds(r0, c), :], b, b_last)

    def lat_chunk(r0, d, st, first):
        b, b_last = decays(lrl_ref, r0, d)
        q = ql_ref[pl.ds(r0, c), :] * scale
        k = kl_ref[pl.ds(r0, c), :]
        v = vl_ref[pl.ds(r0, c), :]
        q_dec = (q * jnp.exp(b)).astype(BF16)
        k_inv = (k * jnp.exp(-b)).astype(BF16)
        vb = v.astype(BF16)
        stb = st.astype(BF16)
        keep = (ci <= ri) if d == 0 else (ci >= ri)
        outs = []
        for h in range(GLA_HEADS):
            ks = slice(GLA_DK * h, GLA_DK * (h + 1))
            vs = slice(GLA_DV * h, GLA_DV * (h + 1))
            sc = lax.dot_general(q_dec[:, ks], k_inv[:, ks], (((1,), (1,)), ((), ())), preferred_element_type=F32)
            sc = jnp.where(keep, sc, 0.0).astype(BF16)
            o_h = jnp.dot(sc, vb[:, vs], preferred_element_type=F32)
            o_h = o_h + lax.dot_general(q_dec[:, ks], stb[:, ks], (((1,), (1,)), ((), ())), preferred_element_type=F32)
            outs.append(o_h)
        o = jnp.concatenate(outs, axis=1)
        if first:
            o_ref[pl.ds(r0, c), :] = o
        else:
            o_ref[pl.ds(r0, c), :] = o_ref[pl.ds(r0, c), :] + o
        return state_update(st, k, v, b, b_last)

    n_lat = t_lat // c
    n_ctx = t_ctx // c
    st0 = jnp.zeros((GLA_DV, qk), F32)
    st = lax.fori_loop(0, n_ctx, lambda i, s: ctx_chunk(pl.multiple_of(i * c, c), 0, s), st0)
    lax.fori_loop(0, n_lat, lambda i, s: lat_chunk(pl.multiple_of(i * c, c), 0, s, True), st)
    st = lax.fori_loop(0, n_ctx, lambda i, s: ctx_chunk(pl.multiple_of((n_ctx - 1 - i) * c, c), 1, s), st0)
    lax.fori_loop(0, n_lat, lambda i, s: lat_chunk(pl.multiple_of((n_lat - 1 - i) * c, c), 1, s, False), st)


def _gla(h_l, h_c, wg_pad, bg, batch, t_lat, t_ctx):
    kern = functools.partial(_gla_kernel, t_lat=t_lat, t_ctx=t_ctx)
    qk = QK_WIDTH
    col = lambda t, width, off: pl.BlockSpec((t, width), lambda b: (b, off // width))
    return pl.pallas_call(
        kern,
        grid=(batch,),
        in_specs=[col(t_lat, qk, COL_Q), col(t_lat, qk, COL_K), col(t_lat, GLA_WIDTH, COL_V), col(t_lat, LANES, COL_LR),
                  col(t_ctx, qk, COL_K), col(t_ctx, GLA_WIDTH, COL_V), col(t_ctx, LANES, COL_LR),
                  pl.BlockSpec((2, LANES, qk), lambda b: (0, 0, 0)), pl.BlockSpec((2, 1, qk), lambda b: (0, 0, 0))],
        out_specs=pl.BlockSpec((t_lat, GLA_WIDTH), lambda b: (b, 0)),
        out_shape=jax.ShapeDtypeStruct((batch * t_lat, GLA_WIDTH), F32),
        compiler_params=_cparams(("arbitrary",)),
        name="gla",
    )(h_l, h_l, h_l, h_l, h_c, h_c, h_c, wg_pad, bg)


def _merge_kernel(y_ref, gate_ref, o_ref, r_ref, x_ref, pos_ref, mod_ref, ng_ref, w_ref, out_ref):
    rg = y_ref[...] * _gelu(gate_ref[...])
    og = o_ref[...]
    heads = [_rms(og[:, GLA_DV * h:GLA_DV * (h + 1)], ng_ref[...]) for h in range(GLA_HEADS)]
    gla = jnp.concatenate(heads, axis=1) * jax.nn.silu(r_ref[...])
    m = jnp.dot(rg.astype(BF16), w_ref[0:RG_WIDTH, :], preferred_element_type=F32)
    m = m + jnp.dot(gla.astype(BF16), w_ref[RG_WIDTH:, :], preferred_element_type=F32)
    out_ref[...] = x_ref[...] + pos_ref[...] + mod_ref[0] * m


def _merge(y_rg, h_l, o_gla, x2, pos, gate_mod, ng, w_out, seq_len):
    n, d = x2.shape
    tm = min(ROW_TILE, seq_len)
    per_seq = seq_len // tm
    w = RG_WIDTH
    return pl.pallas_call(
        _merge_kernel,
        grid=(n // tm,),
        in_specs=[pl.BlockSpec((tm, w), lambda i: (i, 0)),
                  pl.BlockSpec((tm, w), lambda i: (i, COL_RGATE // w)),
                  pl.BlockSpec((tm, GLA_WIDTH), lambda i: (i, 0)),
                  pl.BlockSpec((tm, GLA_WIDTH), lambda i: (i, COL_R // GLA_WIDTH)),
                  pl.BlockSpec((tm, d), lambda i: (i, 0)),
                  pl.BlockSpec((tm, d), lambda i: (i % per_seq, 0)),
                  pl.BlockSpec((1, 1, d), lambda i: (i // per_seq, 0, 0)),
                  pl.BlockSpec((1, GLA_DV), lambda i: (0, 0)),
                  pl.BlockSpec((w + GLA_WIDTH, d), lambda i: (0, 0))],
        out_specs=pl.BlockSpec((tm, d), lambda i: (i, 0)),
        out_shape=jax.ShapeDtypeStruct((n, d), F32),
        compiler_params=_cparams(("arbitrary",)),
        name="merge",
    )(y_rg, h_l, o_gla, h_l, x2, pos, gate_mod, ng, w_out)


def _top16_rows(s, n_rows):
    rows = lax.broadcasted_iota(jnp.int32, s.shape, 0)
    vals, idxs = [], []
    for _ in range(PEER_TOPK):
        m = jnp.max(s, axis=0, keepdims=True)
        idx = jnp.min(jnp.where(s == m, rows, n_rows), axis=0, keepdims=True)
        s = jnp.where(rows == idx, -jnp.inf, s)
        vals.append(m)
        idxs.append(idx)
    return jnp.concatenate(vals, axis=0), jnp.concatenate(idxs, axis=0)


def _select_kernel(x_ref, shift_ref, scale_ref, g_ref, wq_ref, keys_ref, u_ref, e_ref, w_ref, q_scr, ts_scr, ti_scr):
    tt = GATHER_TOKENS
    u = _rms(x_ref[...], g_ref[...]) * (1.0 + scale_ref[0]) + shift_ref[0]
    u_ref[...] = u
    q_scr[...] = jnp.dot(u.astype(BF16), wq_ref[...], preferred_element_type=F32)
    half = PEER_DQ // 2

    def sub_scores(hp, carry):
        q = q_scr[:, pl.ds(pl.multiple_of(hp * half, half), half)].astype(BF16)
        s = lax.dot_general(keys_ref[hp], q, (((1,), (1,)), ((), ())), preferred_element_type=F32)
        v, i = _top16_rows(s, N_KEYS)
        ts_scr[hp] = v
        ti_scr[hp] = i
        return carry

    lax.fori_loop(0, 2 * PEER_HEADS, sub_scores, 0)

    k16 = lax.broadcasted_iota(jnp.int32, (PEER_TOPK, x_ref.shape[0]), 0)

    def head(h, carry):
        s1, s2 = ts_scr[2 * h], ts_scr[2 * h + 1]
        i1, i2 = ti_scr[2 * h], ti_scr[2 * h + 1]
        cand = jnp.concatenate([s1[a:a + 1, :] + s2 for a in range(PEER_TOPK)], axis=0)
        top_s, pos = _top16_rows(cand, PEER_TOPK * PEER_TOPK)
        experts = []
        for k in range(PEER_TOPK):
            p = pos[k:k + 1, :]
            e1 = jnp.sum(jnp.where(k16 == (p >> 4), i1, 0), axis=0, keepdims=True)
            e2 = jnp.sum(jnp.where(k16 == (p & 15), i2, 0), axis=0, keepdims=True)
            experts.append(e1 * N_KEYS + e2)
        ex = jnp.exp(top_s - top_s[0:1, :])
        wts = ex / jnp.sum(ex, axis=0, keepdims=True)
        r0 = pl.multiple_of(h * PEER_TOPK, PEER_TOPK)
        e_ref[pl.ds(r0, PEER_TOPK), :] = jnp.concatenate(experts, axis=0)
        for gidx in range(x_ref.shape[0] // tt):
            w_ref[gidx, pl.ds(r0, PEER_TOPK), :] = wts[:, gidx * tt:(gidx + 1) * tt]
        return carry

    lax.fori_loop(0, PEER_HEADS, head, 0)


def _select(x1, shift, scale, g, wq, keys, seq_len):
    n, d = x1.shape
    tq = min(SEL_TILE, seq_len)
    per_seq = seq_len // tq
    tt = GATHER_TOKENS
    mod_map = lambda i: (i // per_seq, 0, 0)
    return pl.pallas_call(
        _select_kernel,
        grid=(n // tq,),
        in_specs=[pl.BlockSpec((tq, d), lambda i: (i, 0)),
                  pl.BlockSpec((1, 1, d), mod_map), pl.BlockSpec((1, 1, d), mod_map),
                  pl.BlockSpec((1, d), lambda i: (0, 0)),
                  pl.BlockSpec(wq.shape, lambda i: (0, 0)),
                  pl.BlockSpec(keys.shape, lambda i: (0, 0, 0))],
        out_specs=[pl.BlockSpec((tq, d), lambda i: (i, 0)),
                   pl.BlockSpec((N_SEL, tq), lambda i: (0, i)),
                   pl.BlockSpec((tq // tt, N_SEL, tt), lambda i: (i, 0, 0))],
        out_shape=[jax.ShapeDtypeStruct((n, d), F32),
                   jax.ShapeDtypeStruct((N_SEL, n), jnp.int32),
                   jax.ShapeDtypeStruct((n // tt, N_SEL, tt), F32)],
        scratch_shapes=[pltpu.VMEM((tq, PEER_HEADS * PEER_DQ), F32),
                        pltpu.VMEM((2 * PEER_HEADS, PEER_TOPK, tq), F32),
                        pltpu.VMEM((2 * PEER_HEADS, PEER_TOPK, tq), jnp.int32)],
        compiler_params=_cparams(("arbitrary",)),
        name="peer_select",
    )(x1, shift, scale, g, wq, keys)


def _peer_kernel(idx0_ref, idxn_ref, u_ref, w_ref, x_ref, mod_ref, fg_ref, ut_hbm, vt_hbm, out_ref,
                 gu, gv, yacc, sem):
    tt = GATHER_TOKENS
    rows = tt * N_SEL
    i = pl.program_id(0)
    n = pl.num_programs(0)

    def issue(idx_ref, slot):
        def body(r, carry):
            e = idx_ref[0, 0, r]
            pltpu.make_async_copy(ut_hbm.at[pl.ds(e, 1), :], gu.at[slot, pl.ds(r, 1), :], sem.at[0, slot]).start()
            pltpu.make_async_copy(vt_hbm.at[pl.ds(e, 1), :], gv.at[slot, pl.ds(r, 1), :], sem.at[1, slot]).start()
            return carry
        lax.fori_loop(0, rows, body, 0, unroll=8)

    @pl.when(i == 0)
    def _():
        issue(idx0_ref, 0)

    @pl.when(i + 1 < n)
    def _():
        issue(idxn_ref, (i + 1) % 2)

    slot = i % 2
    pltpu.make_async_copy(ut_hbm.at[pl.ds(0, rows), :], gu.at[slot], sem.at[0, slot]).wait()
    pltpu.make_async_copy(vt_hbm.at[pl.ds(0, rows), :], gv.at[slot], sem.at[1, slot]).wait()

    for t in range(tt):
        um = gu[slot, t * N_SEL:(t + 1) * N_SEL, :]
        d = jnp.sum(um * u_ref[t:t + 1, :], axis=1, keepdims=True)
        act = _gelu(d) * w_ref[0, :, t:t + 1]
        vm = gv[slot, t * N_SEL:(t + 1) * N_SEL, :]
        yacc[t:t + 1, :] = jnp.sum(vm * act, axis=0, keepdims=True)

    x2 = x_ref[...] + mod_ref[0] * yacc[...]
    out_ref[...] = _rms(x2, fg_ref[...])


def _peer(idx3, u, wts3, x1, gate_mod, fg, u_tab, v_tab, seq_len):
    n, d = x1.shape
    tt = GATHER_TOKENS
    n_tiles = n // tt
    per_seq = seq_len // tt
    rows = tt * N_SEL
    smem_spec = lambda m: pl.BlockSpec((1, 1, rows), m, memory_space=pltpu.SMEM)
    return pl.pallas_call(
        _peer_kernel,
        grid=(n_tiles,),
        in_specs=[smem_spec(lambda i: (0, 0, 0)),
                  smem_spec(lambda i: (jnp.minimum(i + 1, n_tiles - 1), 0, 0)),
                  pl.BlockSpec((tt, d), lambda i: (i, 0)),
                  pl.BlockSpec((1, N_SEL, tt), lambda i: (i, 0, 0)),
                  pl.BlockSpec((tt, d), lambda i: (i, 0)),
                  pl.BlockSpec((1, 1, d), lambda i: (i // per_seq, 0, 0)),
                  pl.BlockSpec((1, d), lambda i: (0, 0)),
                  pl.BlockSpec(memory_space=pl.ANY),
                  pl.BlockSpec(memory_space=pl.ANY)],
        out_specs=pl.BlockSpec((tt, d), lambda i: (i, 0)),
        out_shape=jax.ShapeDtypeStruct((n, d), F32),
        scratch_shapes=[pltpu.VMEM((2, rows, d), F32), pltpu.VMEM((2, rows, d), F32),
                        pltpu.VMEM((tt, d), F32), pltpu.SemaphoreType.DMA((2, 2))],
        compiler_params=_cparams(("arbitrary",)),
        name="peer_gather",
    )(idx3, idx3, u, wts3, x1, gate_mod, fg, u_tab, v_tab)


def _block_diag(w):
    g, i, j = w.shape
    eye = jnp.eye(g, dtype=w.dtype)
    return (eye[:, None, :, None] * w[:, :, None, :]).reshape(g * i, g * j)


def kernel(x, c, ctx, c_ctx, ada_w, ada_b, norm1_g, w_in, conv_w, conv_b, rg_w_a, rg_b_a, rg_w_x, rg_b_x, rg_lambda, gla_w_g, gla_b_g, gla_norm_g, w_out, norm2_g, peer_w_q, peer_keys, peer_u, peer_v, final_norm_g):
    bsz, t_lat, d = x.shape
    t_ctx = ctx.shape[1]
    assert ada_w.shape[0] == 1, "single-layer block"
    n = bsz * t_lat
    x2 = x.reshape(n, d)
    ctx2 = ctx.reshape(bsz * t_ctx, d)
    pos = _sincos_2d(t_lat, d)

    pad_rows = (-(bsz + 1)) % SUBLANES
    cc = jnp.concatenate([c, c_ctx[None, :], jnp.zeros((pad_rows, d), F32)], axis=0)
    mod = _adaln(cc, ada_w[0], ada_b[0][None, :])
    mod_l = mod[:bsz].reshape(bsz, N_MOD, 1, d)
    mod_c = mod[bsz].reshape(N_MOD, 1, 1, d)

    w_in_p = jnp.pad(w_in[0], ((0, 0), (0, IN_COLS_PAD - w_in.shape[2]))).astype(BF16)
    g1 = norm1_g[0][None, :]
    h_l = _inproj(x2, pos, mod_l[:, 0], mod_l[:, 1], g1, w_in_p, t_lat)
    h_c = _inproj(ctx2, None, mod_c[0], mod_c[1], g1, w_in_p, t_ctx)

    wg_rg = jnp.concatenate([_block_diag(rg_w_a[0, 0]), _block_diag(rg_w_x[0, 0]),
                             _block_diag(rg_w_a[0, 1]), _block_diag(rg_w_x[0, 1])], axis=1).astype(BF16)
    bg_rg = jnp.concatenate([rg_b_a[0, 0], rg_b_x[0, 0], rg_b_a[0, 1], rg_b_x[0, 1]])[None, :]
    y_rg = _rglru(h_l, h_c, conv_w[0], conv_b[0][None, :], wg_rg, bg_rg, rg_lambda[0], bsz, t_lat, t_ctx)

    wg_gla = jnp.zeros((2, LANES, QK_WIDTH), F32)
    wg_gla = wg_gla.at[0, 0:GLA_RANK].set(gla_w_g[0, 0]).at[1, GLA_RANK:2 * GLA_RANK].set(gla_w_g[0, 1])
    o_gla = _gla(h_l, h_c, wg_gla, gla_b_g[0][:, None, :], bsz, t_lat, t_ctx)

    x1 = _merge(y_rg, h_l, o_gla, x2, pos, mod_l[:, 2], gla_norm_g[0][None, :], w_out[0].astype(BF16), t_lat)

    keys = peer_keys[0].reshape(2 * PEER_HEADS, N_KEYS, PEER_DQ // 2).astype(BF16)
    u, eidx, wts3 = _select(x1, mod_l[:, 3], mod_l[:, 4], norm2_g[0][None, :], peer_w_q[0].astype(BF16), keys, t_lat)
    tt = GATHER_TOKENS
    idx3 = eidx.T.reshape(n // tt, 1, tt * N_SEL)
    out = _peer(idx3, u, wts3, x1, mod_l[:, 5], final_norm_g[None, :], peer_u[0], peer_v[0], t_lat)
    return out.reshape(bsz, t_lat, d)
```

```python
import functools
import math

import jax
import jax.numpy as jnp
from jax import lax
from jax.experimental import pallas as pl
from jax.experimental.pallas import tpu as pltpu

F32 = jnp.float32
BF16 = jnp.bfloat16

GRID_W = 64
POS_THETA = 10000.0
RMS_EPS = 1e-6
N_MOD = 6
RG_WIDTH = 512
RG_BLOCKS = 8
RG_C = 8.0
CONV_W = 4
GLA_HEADS = 4
GLA_DK = 64
GLA_DV = 128
GLA_WIDTH = GLA_HEADS * GLA_DV
GLA_RANK = 16
GLA_GATE_NORM = 16.0
GLA_CHUNK = 64
QK_WIDTH = GLA_HEADS * GLA_DK
N_KEYS = 128
PEER_HEADS = 8
PEER_DQ = 256
PEER_TOPK = 16
N_SEL = PEER_HEADS * PEER_TOPK

LANES = 128
SUBLANES = 8
VMEM_LIMIT = 56 * 1024 * 1024

COL_RX, COL_RGATE, COL_Q, COL_K, COL_V, COL_R, COL_LR = 0, 512, 1024, 1280, 1536, 2048, 2560
IN_COLS_PAD = 2688

SCAN_BLOCK = 128
HALO = 8
ROW_TILE = 512
SEL_TILE = 256
GATHER_TOKENS = 8


def _cparams(sem, vmem=VMEM_LIMIT):
    return pltpu.CompilerParams(dimension_semantics=sem, vmem_limit_bytes=vmem)


def _sincos_2d(n_tokens, dim):
    rows = n_tokens // GRID_W
    r, col = jnp.meshgrid(jnp.arange(rows, dtype=F32), jnp.arange(GRID_W, dtype=F32), indexing="ij")
    quarter = dim // 4
    omega = POS_THETA ** (-jnp.arange(quarter, dtype=F32) / quarter)

    def axis_embed(p):
        ang = p.reshape(-1)[:, None] * omega[None, :]
        return jnp.concatenate([jnp.sin(ang), jnp.cos(ang)], axis=-1)

    return jnp.concatenate([axis_embed(r), axis_embed(col)], axis=-1)


def _gelu(x):
    return 0.5 * x * (1.0 + lax.erf(x * (1.0 / math.sqrt(2.0))))


def _rms(x, g):
    ms = jnp.mean(x * x, axis=-1, keepdims=True)
    return x * lax.rsqrt(ms + RMS_EPS) * g


def _adaln_kernel(c_ref, w_ref, b_ref, o_ref):
    a = jax.nn.silu(c_ref[...]).astype(BF16)
    o_ref[...] = jnp.dot(a, w_ref[...].astype(BF16), preferred_element_type=F32) + b_ref[...]


def _adaln(cc, w, b):
    rows, d = cc.shape
    n = w.shape[1]
    tn = 1536
    return pl.pallas_call(
        _adaln_kernel,
        grid=(n // tn,),
        in_specs=[pl.BlockSpec((rows, d), lambda j: (0, 0)),
                  pl.BlockSpec((d, tn), lambda j: (0, j)),
                  pl.BlockSpec((1, tn), lambda j: (0, j))],
        out_specs=pl.BlockSpec((rows, tn), lambda j: (0, j)),
        out_shape=jax.ShapeDtypeStruct((rows, n), F32),
        compiler_params=_cparams(("arbitrary",)),
        name="adaln",
    )(cc, w, b)


def _inproj_kernel(*refs, has_pos):
    if has_pos:
        x_ref, pos_ref, shift_ref, scale_ref, g_ref, w_ref, o_ref = refs
        x = x_ref[...] + pos_ref[...]
    else:
        x_ref, shift_ref, scale_ref, g_ref, w_ref, o_ref = refs
        x = x_ref[...]
    y = _rms(x, g_ref[...]) * (1.0 + scale_ref[0]) + shift_ref[0]
    o_ref[...] = jnp.dot(y.astype(BF16), w_ref[...], preferred_element_type=F32)


def _inproj(x2, pos, shift, scale, g, w, seq_len):
    n, d = x2.shape
    cols = w.shape[1]
    tm = min(ROW_TILE, seq_len)
    per_seq = seq_len // tm
    shared = shift.shape[0] == 1
    mod_map = (lambda i: (0, 0, 0)) if shared else (lambda i: (i // per_seq, 0, 0))
    in_specs = [pl.BlockSpec((tm, d), lambda i: (i, 0))]
    args = [x2]
    if pos is not None:
        in_specs.append(pl.BlockSpec((tm, d), lambda i: (i % per_seq, 0)))
        args.append(pos)
    in_specs += [pl.BlockSpec((1, 1, d), mod_map), pl.BlockSpec((1, 1, d), mod_map),
                 pl.BlockSpec((1, d), lambda i: (0, 0)), pl.BlockSpec((d, cols), lambda i: (0, 0))]
    args += [shift, scale, g, w]
    return pl.pallas_call(
        functools.partial(_inproj_kernel, has_pos=pos is not None),
        grid=(n // tm,),
        in_specs=in_specs,
        out_specs=pl.BlockSpec((tm, cols), lambda i: (i, 0)),
        out_shape=jax.ShapeDtypeStruct((n, cols), F32),
        compiler_params=_cparams(("arbitrary",)),
        name="inproj",
    )(*args)


def _rglru_kernel(rxl_ref, rxc_ref, cw_ref, cb_ref, wg_ref, bg_ref, lam_ref, y_ref, xpl_ref, xpc_ref, *, t_lat, t_ctx):
    tb = SCAN_BLOCK
    w = RG_WIDTH
    zeros_halo = jnp.zeros((HALO, w), F32)
    for xp_ref, src_ref, t in ((xpl_ref, rxl_ref, t_lat), (xpc_ref, rxc_ref, t_ctx)):
        xp_ref[0:HALO, :] = zeros_halo
        xp_ref[HALO:HALO + t, :] = src_ref[...]
        xp_ref[HALO + t:HALO + t + HALO, :] = zeros_halo

    cw = cw_ref[...]
    cb = cb_ref[...]
    row = lax.broadcasted_iota(jnp.int32, (tb, w), 0)

    def block(xp_ref, r0, d, h_in):
        ext = xp_ref[pl.ds(r0, tb + 2 * HALO), :]
        xc = (ext[HALO - 2:HALO - 2 + tb] * cw[0:1] + ext[HALO - 1:HALO - 1 + tb] * cw[1:2]
              + ext[HALO:HALO + tb] * cw[2:3] + ext[HALO + 1:HALO + 1 + tb] * cw[3:4] + cb)
        g = jnp.dot(xc.astype(BF16), wg_ref[:, 2 * w * d:2 * w * (d + 1)], preferred_element_type=F32)
        g = g + bg_ref[:, 2 * w * d:2 * w * (d + 1)]
        gate_r = jax.nn.sigmoid(g[:, :w])
        gate_i = jax.nn.sigmoid(g[:, w:])
        log_a = (-RG_C * jax.nn.softplus(-lam_ref[d:d + 1, :])) * gate_r
        a = jnp.exp(log_a)
        th = jnp.tanh(log_a)
        bv = jnp.sqrt(-2.0 * th / (1.0 - th)) * gate_i * xc
        s = 1
        while s < tb:
            if d == 0:
                valid = row >= s
                a_p = jnp.where(valid, pltpu.roll(a, s, 0), 1.0)
                b_p = jnp.where(valid, pltpu.roll(bv, s, 0), 0.0)
            else:
                valid = row < tb - s
                a_p = jnp.where(valid, pltpu.roll(a, tb - s, 0), 1.0)
                b_p = jnp.where(valid, pltpu.roll(bv, tb - s, 0), 0.0)
            bv = a * b_p + bv
            a = a * a_p
            s *= 2
        h = a * h_in + bv
        h_out = h[tb - 1:tb] if d == 0 else h[0:1]
        return h, h_out

    n_lat = t_lat // tb
    n_ctx = t_ctx // tb
    h0 = jnp.zeros((1, w), F32)

    def ctx_f(i, h):
        return block(xpc_ref, pl.multiple_of(i * tb, tb), 0, h)[1]

    def lat_f(i, h):
        r0 = pl.multiple_of(i * tb, tb)
        hb, h_out = block(xpl_ref, r0, 0, h)
        y_ref[pl.ds(r0, tb), :] = hb
        return h_out

    h = lax.fori_loop(0, n_ctx, ctx_f, h0)
    lax.fori_loop(0, n_lat, lat_f, h)

    def ctx_b(i, h):
        return block(xpc_ref, pl.multiple_of((n_ctx - 1 - i) * tb, tb), 1, h)[1]

    def lat_b(i, h):
        r0 = pl.multiple_of((n_lat - 1 - i) * tb, tb)
        hb, h_out = block(xpl_ref, r0, 1, h)
        y_ref[pl.ds(r0, tb), :] = y_ref[pl.ds(r0, tb), :] + hb
        return h_out

    h = lax.fori_loop(0, n_ctx, ctx_b, h0)
    lax.fori_loop(0, n_lat, lat_b, h)


def _rglru(h_l, h_c, conv_w, conv_b, wg, bg, lam, batch, t_lat, t_ctx):
    w = RG_WIDTH
    kern = functools.partial(_rglru_kernel, t_lat=t_lat, t_ctx=t_ctx)
    full = lambda shape: pl.BlockSpec(shape, lambda b: tuple(0 for _ in shape))
    return pl.pallas_call(
        kern,
        grid=(batch,),
        in_specs=[pl.BlockSpec((t_lat, w), lambda b: (b, COL_RX // w)),
                  pl.BlockSpec((t_ctx, w), lambda b: (b, COL_RX // w)),
                  full((CONV_W, w)), full((1, w)), full((w, 4 * w)), full((1, 4 * w)), full((2, w))],
        out_specs=pl.BlockSpec((t_lat, w), lambda b: (b, 0)),
        out_shape=jax.ShapeDtypeStruct((batch * t_lat, w), F32),
        scratch_shapes=[pltpu.VMEM((t_lat + 2 * HALO, w), F32), pltpu.VMEM((t_ctx + 2 * HALO, w), F32)],
        compiler_params=_cparams(("arbitrary",)),
        name="rglru",
    )(h_l, h_c, conv_w, conv_b, wg, bg, lam)


def _gla_kernel(ql_ref, kl_ref, vl_ref, lrl_ref, kc_ref, vc_ref, lrc_ref, wg_ref, bg_ref, o_ref, *, t_lat, t_ctx):
    c = GLA_CHUNK
    qk = QK_WIDTH
    scale = GLA_DK ** -0.5
    row = lax.broadcasted_iota(jnp.int32, (c, qk), 0)
    ri = lax.broadcasted_iota(jnp.int32, (c, c), 0)
    ci = lax.broadcasted_iota(jnp.int32, (c, c), 1)

    def decays(lr_ref, r0, d):
        z = jnp.dot(lr_ref[pl.ds(r0, c), :].astype(BF16), wg_ref[d].astype(BF16), preferred_element_type=F32) + bg_ref[d]
        b = jax.nn.log_sigmoid(z) * (1.0 / GLA_GATE_NORM)
        s = 1
        while s < c:
            if d == 0:
                b = b + jnp.where(row >= s, pltpu.roll(b, s, 0), 0.0)
            else:
                b = b + jnp.where(row < c - s, pltpu.roll(b, c - s, 0), 0.0)
            s *= 2
        b_last = b[c - 1:c] if d == 0 else b[0:1]
        return b, b_last

    def state_update(st, k, v, b, b_last):
        k_dec = (k * jnp.exp(b_last - b)).astype(BF16)
        vb = v.astype(BF16)
        parts = []
        for h in range(GLA_HEADS):
            parts.append(lax.dot_general(vb[:, GLA_DV * h:GLA_DV * (h + 1)], k_dec[:, GLA_DK * h:GLA_DK * (h + 1)],
                                         (((0,), (0,)), ((), ())), preferred_element_type=F32))
        return st * jnp.exp(b_last) + jnp.concatenate(parts, axis=1)

    def ctx_chunk(r0, d, st):
        b, b_last = decays(lrc_ref, r0, d)
        return state_update(st, kc_ref[pl.ds(r0, c), :], vc_ref[pl.ds(r0, c), :], b, b_last)

    def lat_chunk(r0, d, st, first):
        b, b_last = decays(lrl_ref, r0, d)
        q = ql_ref[pl.ds(r0, c), :] * scale
        k = kl_ref[pl.ds(r0, c), :]
        v = vl_ref[pl.ds(r0, c), :]
        q_dec = (q * jnp.exp(b)).astype(BF16)
        k_inv = (k * jnp.exp(-b)).astype(BF16)
        vb = v.astype(BF16)
        stb = st.astype(BF16)
        keep = (ci <= ri) if d == 0 else (ci >= ri)
        outs = []
        for h in range(GLA_HEADS):
            ks = slice(GLA_DK * h, GLA_DK * (h + 1))
            vs = slice(GLA_DV * h, GLA_DV * (h + 1))
            sc = lax.dot_general(q_dec[:, ks], k_inv[:, ks], (((1,), (1,)), ((), ())), preferred_element_type=F32)
            sc = jnp.where(keep, sc, 0.0).astype(BF16)
            o_h = jnp.dot(sc, vb[:, vs], preferred_element_type=F32)
            o_h = o_h + lax.dot_general(q_dec[:, ks], stb[:, ks], (((1,), (1,)), ((), ())), preferred_element_type=F32)
            outs.append(o_h)
        o = jnp.concatenate(outs, axis=1)
        if first:
            o_ref[pl.ds(r0, c), :] = o
        else:
            o_ref[pl.ds(r0, c), :] = o_ref[pl.ds(r0, c), :] + o
        return state_update(st, k, v, b, b_last)

    n_lat = t_lat // c
    n_ctx = t_ctx // c
    st0 = jnp.zeros((GLA_DV, qk), F32)
    st = lax.fori_loop(0, n_ctx, lambda i, s: ctx_chunk(pl.multiple_of(i * c, c), 0, s), st0)
    lax.fori_loop(0, n_lat, lambda i, s: lat_chunk(pl.multiple_of(i * c, c), 0, s, True), st)
    st = lax.fori_loop(0, n_ctx, lambda i, s: ctx_chunk(pl.multiple_of((n_ctx - 1 - i) * c, c), 1, s), st0)
    lax.fori_loop(0, n_lat, lambda i, s: lat_chunk(pl.multiple_of((n_lat - 1 - i) * c, c), 1, s, False), st)


def _gla(h_l, h_c, wg_pad, bg, batch, t_lat, t_ctx):
    kern = functools.partial(_gla_kernel, t_lat=t_lat, t_ctx=t_ctx)
    qk = QK_WIDTH
    col = lambda t, width, off: pl.BlockSpec((t, width), lambda b: (b, off // width))
    return pl.pallas_call(
        kern,
        grid=(batch,),
        in_specs=[col(t_lat, qk, COL_Q), col(t_lat, qk, COL_K), col(t_lat, GLA_WIDTH, COL_V), col(t_lat, LANES, COL_LR),
                  col(t_ctx, qk, COL_K), col(t_ctx, GLA_WIDTH, COL_V), col(t_ctx, LANES, COL_LR),
                  pl.BlockSpec((2, LANES, qk), lambda b: (0, 0, 0)), pl.BlockSpec((2, 1, qk), lambda b: (0, 0, 0))],
        out_specs=pl.BlockSpec((t_lat, GLA_WIDTH), lambda b: (b, 0)),
        out_shape=jax.ShapeDtypeStruct((batch * t_lat, GLA_WIDTH), F32),
        compiler_params=_cparams(("arbitrary",)),
        name="gla",
    )(h_l, h_l, h_l, h_l, h_c, h_c, h_c, wg_pad, bg)


def _merge_kernel(y_ref, gate_ref, o_ref, r_ref, x_ref, pos_ref, mod_ref, ng_ref, w_ref, out_ref):
    rg = y_ref[...] * _gelu(gate_ref[...])
    og = o_ref[...]
    heads = [_rms(og[:, GLA_DV * h:GLA_DV * (h + 1)], ng_ref[...]) for h in range(GLA_HEADS)]
    gla = jnp.concatenate(heads, axis=1) * jax.nn.silu(r_ref[...])
    m = jnp.dot(rg.astype(BF16), w_ref[0:RG_WIDTH, :], preferred_element_type=F32)
    m = m + jnp.dot(gla.astype(BF16), w_ref[RG_WIDTH:, :], preferred_element_type=F32)
    out_ref[...] = x_ref[...] + pos_ref[...] + mod_ref[0] * m


def _merge(y_rg, h_l, o_gla, x2, pos, gate_mod, ng, w_out, seq_len):
    n, d = x2.shape
    tm = min(ROW_TILE, seq_len)
    per_seq = seq_len // tm
    w = RG_WIDTH
    return pl.pallas_call(
        _merge_kernel,
        grid=(n // tm,),
        in_specs=[pl.BlockSpec((tm, w), lambda i: (i, 0)),
                  pl.BlockSpec((tm, w), lambda i: (i, COL_RGATE // w)),
                  pl.BlockSpec((tm, GLA_WIDTH), lambda i: (i, 0)),
                  pl.BlockSpec((tm, GLA_WIDTH), lambda i: (i, COL_R // GLA_WIDTH)),
                  pl.BlockSpec((tm, d), lambda i: (i, 0)),
                  pl.BlockSpec((tm, d), lambda i: (i % per_seq, 0)),
                  pl.BlockSpec((1, 1, d), lambda i: (i // per_seq, 0, 0)),
                  pl.BlockSpec((1, GLA_DV), lambda i: (0, 0)),
                  pl.BlockSpec((w + GLA_WIDTH, d), lambda i: (0, 0))],
        out_specs=pl.BlockSpec((tm, d), lambda i: (i, 0)),
        out_shape=jax.ShapeDtypeStruct((n, d), F32),
        compiler_params=_cparams(("arbitrary",)),
        name="merge",
    )(y_rg, h_l, o_gla, h_l, x2, pos, gate_mod, ng, w_out)


def _top16_rows(s, n_rows):
    rows = lax.broadcasted_iota(jnp.int32, s.shape, 0)
    vals, idxs = [], []
    for _ in range(PEER_TOPK):
        m = jnp.max(s, axis=0, keepdims=True)
        idx = jnp.min(jnp.where(s == m, rows, n_rows), axis=0, keepdims=True)
        s = jnp.where(rows == idx, -jnp.inf, s)
        vals.append(m)
        idxs.append(idx)
    return jnp.concatenate(vals, axis=0), jnp.concatenate(idxs, axis=0)


def _select_kernel(x_ref, shift_ref, scale_ref, g_ref, wq_ref, keys_ref, u_ref, e_ref, w_ref, q_scr, ts_scr, ti_scr):
    tt = GATHER_TOKENS
    u = _rms(x_ref[...], g_ref[...]) * (1.0 + scale_ref[0]) + shift_ref[0]
    u_ref[...] = u
    q_scr[...] = jnp.dot(u.astype(BF16), wq_ref[...], preferred_element_type=F32)
    half = PEER_DQ // 2

    def sub_scores(hp, carry):
        q = q_scr[:, pl.ds(pl.multiple_of(hp * half, half), half)].astype(BF16)
        s = lax.dot_general(keys_ref[hp], q, (((1,), (1,)), ((), ())), preferred_element_type=F32)
        v, i = _top16_rows(s, N_KEYS)
        ts_scr[hp] = v
        ti_scr[hp] = i
        return carry

    lax.fori_loop(0, 2 * PEER_HEADS, sub_scores, 0)

    k16 = lax.broadcasted_iota(jnp.int32, (PEER_TOPK, x_ref.shape[0]), 0)

    def head(h, carry):
        s1, s2 = ts_scr[2 * h], ts_scr[2 * h + 1]
        i1, i2 = ti_scr[2 * h], ti_scr[2 * h + 1]
        cand = jnp.concatenate([s1[a:a + 1, :] + s2 for a in range(PEER_TOPK)], axis=0)
        top_s, pos = _top16_rows(cand, PEER_TOPK * PEER_TOPK)
        experts = []
        for k in range(PEER_TOPK):
            p = pos[k:k + 1, :]
            e1 = jnp.sum(jnp.where(k16 == (p >> 4), i1, 0), axis=0, keepdims=True)
            e2 = jnp.sum(jnp.where(k16 == (p & 15), i2, 0), axis=0, keepdims=True)
            experts.append(e1 * N_KEYS + e2)
        ex = jnp.exp(top_s - top_s[0:1, :])
        wts = ex / jnp.sum(ex, axis=0, keepdims=True)
        r0 = pl.multiple_of(h * PEER_TOPK, PEER_TOPK)
        e_ref[pl.ds(r0, PEER_TOPK), :] = jnp.concatenate(experts, axis=0)
        for gidx in range(x_ref.shape[0] // tt):
            w_ref[gidx, pl.ds(r0, PEER_TOPK), :] = wts[:, gidx * tt:(gidx + 1) * tt]
        return carry

    lax.fori_loop(0, PEER_HEADS, head, 0)


def _select(x1, shift, scale, g, wq, keys, seq_len):
    n, d = x1.shape
    tq = min(SEL_TILE, seq_len)
    per_seq = seq_len // tq
    tt = GATHER_TOKENS
    mod_map = lambda i: (i // per_seq, 0, 0)
    return pl.pallas_call(
        _select_kernel,
        grid=(n // tq,),
        in_specs=[pl.BlockSpec((tq, d), lambda i: (i, 0)),
                  pl.BlockSpec((1, 1, d), mod_map), pl.BlockSpec((1, 1, d), mod_map),
                  pl.BlockSpec((1, d), lambda i: (0, 0)),
                  pl.BlockSpec(wq.shape, lambda i: (0, 0)),
                  pl.BlockSpec(keys.shape, lambda i: (0, 0, 0))],
        out_specs=[pl.BlockSpec((tq, d), lambda i: (i, 0)),
                   pl.BlockSpec((N_SEL, tq), lambda i: (0, i)),
                   pl.BlockSpec((tq // tt, N_SEL, tt), lambda i: (i, 0, 0))],
        out_shape=[jax.ShapeDtypeStruct((n, d), F32),
                   jax.ShapeDtypeStruct((N_SEL, n), jnp.int32),
                   jax.ShapeDtypeStruct((n // tt, N_SEL, tt), F32)],
        scratch_shapes=[pltpu.VMEM((tq, PEER_HEADS * PEER_DQ), F32),
                        pltpu.VMEM((2 * PEER_HEADS, PEER_TOPK, tq), F32),
                        pltpu.VMEM((2 * PEER_HEADS, PEER_TOPK, tq), jnp.int32)],
        compiler_params=_cparams(("arbitrary",)),
        name="peer_select",
    )(x1, shift, scale, g, wq, keys)


def _fold_sublanes(p):
    sub = lax.broadcasted_iota(jnp.int32, p.shape[2:], 0)
    half = SUBLANES // 2
    while half >= 1:
        k = p.shape[1] // 2
        a, b = p[:, :k], p[:, k:]
        low = (sub & half) == 0
        p = jnp.where(low, a, b) + jnp.where(low, pltpu.roll(a, SUBLANES - half, 2), pltpu.roll(b, half, 2))
        half //= 2
    return p[:, 0]


def _peer_kernel(idx0_ref, idxn_ref, u_ref, w_ref, x_ref, mod_ref, fg_ref, tab_hbm, out_ref, gbuf, xs, yacc, sem):
    tt = GATHER_TOKENS
    rows = tt * N_SEL
    d = x_ref.shape[1]
    s8 = SUBLANES
    groups = N_SEL // s8
    i = pl.program_id(0)
    n = pl.num_programs(0)

    def issue(idx_ref, slot, r_lo, r_hi):
        for r in range(r_lo, r_hi):
            e = idx_ref[0, 0, r]
            pltpu.make_async_copy(tab_hbm.at[pl.ds(e, 1)], gbuf.at[slot, pl.ds(r, 1)], sem.at[slot]).start(priority=r % 2)

    def wait_slot(slot):
        pltpu.make_async_copy(tab_hbm.at[pl.ds(0, rows)], gbuf.at[slot], sem.at[slot]).wait()

    @pl.when(i == 0)
    def _():
        issue(idx0_ref, 0, 0, rows)

    def step(cur, nxt):
        for t in range(tt):
            for s in range(s8):
                xs[t, s:s + 1, :] = u_ref[t:t + 1, LANES * s:LANES * (s + 1)]
        wait_slot(cur)
        half = s8 // 2
        for t in range(tt):
            xt = xs[t]
            dots = []
            for jb in range(groups):
                r0 = t * N_SEL + jb * s8
                issue(idxn_ref, nxt, r0, r0 + half)
                p = gbuf[cur, r0:r0 + s8, 0:s8, :] * xt
                dots.append(jnp.sum(_fold_sublanes(p[None]), axis=-1, keepdims=True)[0])
            wcol = w_ref[0, :, t:t + 1]
            y = jnp.zeros((s8, LANES), F32)
            for jb in range(groups):
                r0 = t * N_SEL + jb * s8
                issue(idxn_ref, nxt, r0 + half, r0 + s8)
                act = _gelu(dots[jb]) * wcol[jb * s8:(jb + 1) * s8, :]
                v = gbuf[cur, r0:r0 + s8, s8:2 * s8, :]
                for jj in range(s8):
                    y = y + v[jj] * act[jj:jj + 1, :]
            for s in range(s8):
                yacc[t:t + 1, LANES * s:LANES * (s + 1)] = y[s:s + 1, :]

    @pl.when(i % 2 == 0)
    def _():
        step(0, 1)

    @pl.when(i % 2 == 1)
    def _():
        step(1, 0)

    @pl.when(i == n - 1)
    def _():
        wait_slot((i + 1) % 2)

    x2 = x_ref[...] + mod_ref[0] * yacc[...]
    out_ref[...] = _rms(x2, fg_ref[...])


def _peer(idx3, u, wts3, x1, gate_mod, fg, uv_tab, seq_len):
    n, d = x1.shape
    tt = GATHER_TOKENS
    n_tiles = n // tt
    per_seq = seq_len // tt
    rows = tt * N_SEL
    smem_spec = lambda m: pl.BlockSpec((1, 1, rows), m, memory_space=pltpu.SMEM)
    return pl.pallas_call(
        _peer_kernel,
        grid=(n_tiles,),
        in_specs=[smem_spec(lambda i: (0, 0, 0)),
                  smem_spec(lambda i: (jnp.minimum(i + 1, n_tiles - 1), 0, 0)),
                  pl.BlockSpec((tt, d), lambda i: (i, 0)),
                  pl.BlockSpec((1, N_SEL, tt), lambda i: (i, 0, 0)),
                  pl.BlockSpec((tt, d), lambda i: (i, 0)),
                  pl.BlockSpec((1, 1, d), lambda i: (i // per_seq, 0, 0)),
                  pl.BlockSpec((1, d), lambda i: (0, 0)),
                  pl.BlockSpec(memory_space=pl.ANY)],
        out_specs=pl.BlockSpec((tt, d), lambda i: (i, 0)),
        out_shape=jax.ShapeDtypeStruct((n, d), F32),
        scratch_shapes=[pltpu.VMEM((2, rows) + uv_tab.shape[1:], F32), pltpu.VMEM((tt, SUBLANES, LANES), F32),
                        pltpu.VMEM((tt, d), F32), pltpu.SemaphoreType.DMA((2,))],
        compiler_params=_cparams(("arbitrary",)),
        name="peer_gather",
    )(idx3, idx3, u, wts3, x1, gate_mod, fg, uv_tab)


def _block_diag(w):
    g, i, j = w.shape
    eye = jnp.eye(g, dtype=w.dtype)
    return (eye[:, None, :, None] * w[:, :, None, :]).reshape(g * i, g * j)


def kernel(x, c, ctx, c_ctx, ada_w, ada_b, norm1_g, w_in, conv_w, conv_b, rg_w_a, rg_b_a, rg_w_x, rg_b_x, rg_lambda, gla_w_g, gla_b_g, gla_norm_g, w_out, norm2_g, peer_w_q, peer_keys, peer_u, peer_v, final_norm_g):
    bsz, t_lat, d = x.shape
    t_ctx = ctx.shape[1]
    assert ada_w.shape[0] == 1, "single-layer block"
    n = bsz * t_lat
    x2 = x.reshape(n, d)
    ctx2 = ctx.reshape(bsz * t_ctx, d)
    pos = _sincos_2d(t_lat, d)

    pad_rows = (-(bsz + 1)) % SUBLANES
    cc = jnp.concatenate([c, c_ctx[None, :], jnp.zeros((pad_rows, d), F32)], axis=0)
    mod = _adaln(cc, ada_w[0], ada_b[0][None, :])
    mod_l = mod[:bsz].reshape(bsz, N_MOD, 1, d)
    mod_c = mod[bsz].reshape(N_MOD, 1, 1, d)

    w_in_p = jnp.pad(w_in[0], ((0, 0), (0, IN_COLS_PAD - w_in.shape[2]))).astype(BF16)
    g1 = norm1_g[0][None, :]
    h_l = _inproj(x2, pos, mod_l[:, 0], mod_l[:, 1], g1, w_in_p, t_lat)
    h_c = _inproj(ctx2, None, mod_c[0], mod_c[1], g1, w_in_p, t_ctx)

    wg_rg = jnp.concatenate([_block_diag(rg_w_a[0, 0]), _block_diag(rg_w_x[0, 0]),
                             _block_diag(rg_w_a[0, 1]), _block_diag(rg_w_x[0, 1])], axis=1).astype(BF16)
    bg_rg = jnp.concatenate([rg_b_a[0, 0], rg_b_x[0, 0], rg_b_a[0, 1], rg_b_x[0, 1]])[None, :]
    y_rg = _rglru(h_l, h_c, conv_w[0], conv_b[0][None, :], wg_rg, bg_rg, rg_lambda[0], bsz, t_lat, t_ctx)

    wg_gla = jnp.zeros((2, LANES, QK_WIDTH), F32)
    wg_gla = wg_gla.at[0, 0:GLA_RANK].set(gla_w_g[0, 0]).at[1, GLA_RANK:2 * GLA_RANK].set(gla_w_g[0, 1])
    o_gla = _gla(h_l, h_c, wg_gla, gla_b_g[0][:, None, :], bsz, t_lat, t_ctx)

    x1 = _merge(y_rg, h_l, o_gla, x2, pos, mod_l[:, 2], gla_norm_g[0][None, :], w_out[0].astype(BF16), t_lat)

    keys = peer_keys[0].reshape(2 * PEER_HEADS, N_KEYS, PEER_DQ // 2).astype(BF16)
    u, eidx, wts3 = _select(x1, mod_l[:, 3], mod_l[:, 4], norm2_g[0][None, :], peer_w_q[0].astype(BF16), keys, t_lat)
    tt = GATHER_TOKENS
    idx3 = eidx.T.reshape(n // tt, 1, tt * N_SEL)
    uv_tab = jnp.concatenate([peer_u[0], peer_v[0]], axis=1).reshape(peer_u.shape[1], 2 * d // LANES, LANES)
    out = _peer(idx3, u, wts3, x1, mod_l[:, 5], final_norm_g[None, :], uv_tab, t_lat)
    return out.reshape(bsz, t_lat, d)
```

```python
import functools
import math

import jax
import jax.numpy as jnp
from jax import lax
from jax.experimental import pallas as pl
from jax.experimental.pallas import tpu as pltpu

F32 = jnp.float32
BF16 = jnp.bfloat16

GRID_W = 64
POS_THETA = 10000.0
RMS_EPS = 1e-6
N_MOD = 6
RG_WIDTH = 512
RG_BLOCKS = 8
RG_C = 8.0
CONV_W = 4
GLA_HEADS = 4
GLA_DK = 64
GLA_DV = 128
GLA_WIDTH = GLA_HEADS * GLA_DV
GLA_RANK = 16
GLA_GATE_NORM = 16.0
GLA_CHUNK = 64
QK_WIDTH = GLA_HEADS * GLA_DK
N_KEYS = 128
PEER_HEADS = 8
PEER_DQ = 256
PEER_TOPK = 16
N_SEL = PEER_HEADS * PEER_TOPK

LANES = 128
SUBLANES = 8
VMEM_LIMIT = 56 * 1024 * 1024

COL_RX, COL_RGATE, COL_Q, COL_K, COL_V, COL_R, COL_LR = 0, 512, 1024, 1280, 1536, 2048, 2560
IN_COLS_PAD = 2688

SCAN_BLOCK = 128
HALO = 8
ROW_TILE = 512
SEL_TILE = 256
GATHER_TOKENS = 8


def _cparams(sem, vmem=VMEM_LIMIT):
    return pltpu.CompilerParams(dimension_semantics=sem, vmem_limit_bytes=vmem)


def _sincos_2d(n_tokens, dim):
    rows = n_tokens // GRID_W
    r, col = jnp.meshgrid(jnp.arange(rows, dtype=F32), jnp.arange(GRID_W, dtype=F32), indexing="ij")
    quarter = dim // 4
    omega = POS_THETA ** (-jnp.arange(quarter, dtype=F32) / quarter)

    def axis_embed(p):
        ang = p.reshape(-1)[:, None] * omega[None, :]
        return jnp.concatenate([jnp.sin(ang), jnp.cos(ang)], axis=-1)

    return jnp.concatenate([axis_embed(r), axis_embed(col)], axis=-1)


def _gelu(x):
    return 0.5 * x * (1.0 + lax.erf(x * (1.0 / math.sqrt(2.0))))


def _rms(x, g):
    ms = jnp.mean(x * x, axis=-1, keepdims=True)
    return x * lax.rsqrt(ms + RMS_EPS) * g


def _adaln_kernel(c_ref, w_ref, b_ref, o_ref):
    a = jax.nn.silu(c_ref[...]).astype(BF16)
    o_ref[...] = jnp.dot(a, w_ref[...].astype(BF16), preferred_element_type=F32) + b_ref[...]


def _adaln(cc, w, b):
    rows, d = cc.shape
    n = w.shape[1]
    tn = 1536
    return pl.pallas_call(
        _adaln_kernel,
        grid=(n // tn,),
        in_specs=[pl.BlockSpec((rows, d), lambda j: (0, 0)),
                  pl.BlockSpec((d, tn), lambda j: (0, j)),
                  pl.BlockSpec((1, tn), lambda j: (0, j))],
        out_specs=pl.BlockSpec((rows, tn), lambda j: (0, j)),
        out_shape=jax.ShapeDtypeStruct((rows, n), F32),
        compiler_params=_cparams(("arbitrary",)),
        name="adaln",
    )(cc, w, b)


def _inproj_kernel(*refs, has_pos):
    if has_pos:
        x_ref, pos_ref, shift_ref, scale_ref, g_ref, w_ref, o_ref = refs
        x = x_ref[...] + pos_ref[...]
    else:
        x_ref, shift_ref, scale_ref, g_ref, w_ref, o_ref = refs
        x = x_ref[...]
    y = _rms(x, g_ref[...]) * (1.0 + scale_ref[0]) + shift_ref[0]
    o_ref[...] = jnp.dot(y.astype(BF16), w_ref[...], preferred_element_type=F32)


def _inproj(x2, pos, shift, scale, g, w, seq_len):
    n, d = x2.shape
    cols = w.shape[1]
    tm = min(ROW_TILE, seq_len)
    per_seq = seq_len // tm
    shared = shift.shape[0] == 1
    mod_map = (lambda i: (0, 0, 0)) if shared else (lambda i: (i // per_seq, 0, 0))
    in_specs = [pl.BlockSpec((tm, d), lambda i: (i, 0))]
    args = [x2]
    if pos is not None:
        in_specs.append(pl.BlockSpec((tm, d), lambda i: (i % per_seq, 0)))
        args.append(pos)
    in_specs += [pl.BlockSpec((1, 1, d), mod_map), pl.BlockSpec((1, 1, d), mod_map),
                 pl.BlockSpec((1, d), lambda i: (0, 0)), pl.BlockSpec((d, cols), lambda i: (0, 0))]
    args += [shift, scale, g, w]
    return pl.pallas_call(
        functools.partial(_inproj_kernel, has_pos=pos is not None),
        grid=(n // tm,),
        in_specs=in_specs,
        out_specs=pl.BlockSpec((tm, cols), lambda i: (i, 0)),
        out_shape=jax.ShapeDtypeStruct((n, cols), F32),
        compiler_params=_cparams(("arbitrary",)),
        name="inproj",
    )(*args)


def _rglru_kernel(rxl_ref, rxc_ref, cw_ref, cb_ref, wg_ref, bg_ref, lam_ref, y_ref, xpl_ref, xpc_ref, *, t_lat, t_ctx):
    tb = SCAN_BLOCK
    w = RG_WIDTH
    zeros_halo = jnp.zeros((HALO, w), F32)
    for xp_ref, src_ref, t in ((xpl_ref, rxl_ref, t_lat), (xpc_ref, rxc_ref, t_ctx)):
        xp_ref[0:HALO, :] = zeros_halo
        xp_ref[HALO:HALO + t, :] = src_ref[...]
        xp_ref[HALO + t:HALO + t + HALO, :] = zeros_halo

    cw = cw_ref[...]
    cb = cb_ref[...]
    row = lax.broadcasted_iota(jnp.int32, (tb, w), 0)

    def block(xp_ref, r0, d, h_in):
        ext = xp_ref[pl.ds(r0, tb + 2 * HALO), :]
        xc = (ext[HALO - 2:HALO - 2 + tb] * cw[0:1] + ext[HALO - 1:HALO - 1 + tb] * cw[1:2]
              + ext[HALO:HALO + tb] * cw[2:3] + ext[HALO + 1:HALO + 1 + tb] * cw[3:4] + cb)
        g = jnp.dot(xc.astype(BF16), wg_ref[:, 2 * w * d:2 * w * (d + 1)], preferred_element_type=F32)
        g = g + bg_ref[:, 2 * w * d:2 * w * (d + 1)]
        gate_r = jax.nn.sigmoid(g[:, :w])
        gate_i = jax.nn.sigmoid(g[:, w:])
        log_a = (-RG_C * jax.nn.softplus(-lam_ref[d:d + 1, :])) * gate_r
        a = jnp.exp(log_a)
        th = jnp.tanh(log_a)
        bv = jnp.sqrt(-2.0 * th / (1.0 - th)) * gate_i * xc
        s = 1
        while s < tb:
            if d == 0:
                valid = row >= s
                a_p = jnp.where(valid, pltpu.roll(a, s, 0), 1.0)
                b_p = jnp.where(valid, pltpu.roll(bv, s, 0), 0.0)
            else:
                valid = row < tb - s
                a_p = jnp.where(valid, pltpu.roll(a, tb - s, 0), 1.0)
                b_p = jnp.where(valid, pltpu.roll(bv, tb - s, 0), 0.0)
            bv = a * b_p + bv
            a = a * a_p
            s *= 2
        h = a * h_in + bv
        h_out = h[tb - 1:tb] if d == 0 else h[0:1]
        return h, h_out

    n_lat = t_lat // tb
    n_ctx = t_ctx // tb
    h0 = jnp.zeros((1, w), F32)

    def ctx_f(i, h):
        return block(xpc_ref, pl.multiple_of(i * tb, tb), 0, h)[1]

    def lat_f(i, h):
        r0 = pl.multiple_of(i * tb, tb)
        hb, h_out = block(xpl_ref, r0, 0, h)
        y_ref[pl.ds(r0, tb), :] = hb
        return h_out

    h = lax.fori_loop(0, n_ctx, ctx_f, h0)
    lax.fori_loop(0, n_lat, lat_f, h)

    def ctx_b(i, h):
        return block(xpc_ref, pl.multiple_of((n_ctx - 1 - i) * tb, tb), 1, h)[1]

    def lat_b(i, h):
        r0 = pl.multiple_of((n_lat - 1 - i) * tb, tb)
        hb, h_out = block(xpl_ref, r0, 1, h)
        y_ref[pl.ds(r0, tb), :] = y_ref[pl.ds(r0, tb), :] + hb
        return h_out

    h = lax.fori_loop(0, n_ctx, ctx_b, h0)
    lax.fori_loop(0, n_lat, lat_b, h)


def _rglru(h_l, h_c, conv_w, conv_b, wg, bg, lam, batch, t_lat, t_ctx):
    w = RG_WIDTH
    kern = functools.partial(_rglru_kernel, t_lat=t_lat, t_ctx=t_ctx)
    full = lambda shape: pl.BlockSpec(shape, lambda b: tuple(0 for _ in shape))
    return pl.pallas_call(
        kern,
        grid=(batch,),
        in_specs=[pl.BlockSpec((t_lat, w), lambda b: (b, COL_RX // w)),
                  pl.BlockSpec((t_ctx, w), lambda b: (b, COL_RX // w)),
                  full((CONV_W, w)), full((1, w)), full((w, 4 * w)), full((1, 4 * w)), full((2, w))],
        out_specs=pl.BlockSpec((t_lat, w), lambda b: (b, 0)),
        out_shape=jax.ShapeDtypeStruct((batch * t_lat, w), F32),
        scratch_shapes=[pltpu.VMEM((t_lat + 2 * HALO, w), F32), pltpu.VMEM((t_ctx + 2 * HALO, w), F32)],
        compiler_params=_cparams(("arbitrary",)),
        name="rglru",
    )(h_l, h_c, conv_w, conv_b, wg, bg, lam)


def _gla_kernel(ql_ref, kl_ref, vl_ref, lrl_ref, kc_ref, vc_ref, lrc_ref, wg_ref, bg_ref, o_ref, *, t_lat, t_ctx):
    c = GLA_CHUNK
    qk = QK_WIDTH
    scale = GLA_DK ** -0.5
    row = lax.broadcasted_iota(jnp.int32, (c, qk), 0)
    ri = lax.broadcasted_iota(jnp.int32, (c, c), 0)
    ci = lax.broadcasted_iota(jnp.int32, (c, c), 1)

    def decays(lr_ref, r0, d):
        z = jnp.dot(lr_ref[pl.ds(r0, c), :].astype(BF16), wg_ref[d].astype(BF16), preferred_element_type=F32) + bg_ref[d]
        b = jax.nn.log_sigmoid(z) * (1.0 / GLA_GATE_NORM)
        s = 1
        while s < c:
            if d == 0:
                b = b + jnp.where(row >= s, pltpu.roll(b, s, 0), 0.0)
            else:
                b = b + jnp.where(row < c - s, pltpu.roll(b, c - s, 0), 0.0)
            s *= 2
        b_last = b[c - 1:c] if d == 0 else b[0:1]
        return b, b_last

    def state_update(st, k, v, b, b_last):
        k_dec = (k * jnp.exp(b_last - b)).astype(BF16)
        vb = v.astype(BF16)
        parts = []
        for h in range(GLA_HEADS):
            parts.append(lax.dot_general(vb[:, GLA_DV * h:GLA_DV * (h + 1)], k_dec[:, GLA_DK * h:GLA_DK * (h + 1)],
                                         (((0,), (0,)), ((), ())), preferred_element_type=F32))
        return st * jnp.exp(b_last) + jnp.concatenate(parts, axis=1)

    def ctx_chunk(r0, d, st):
        b, b_last = decays(lrc_ref, r0, d)
        return state_update(st, kc_ref[pl.ds(r0, c), :], vc_ref[pl.ds(r0, c), :], b, b_last)

    def lat_chunk(r0, d, st, first):
        b, b_last = decays(lrl_ref, r0, d)
        q = ql_ref[pl.ds(r0, c), :] * scale
        k = kl_ref[pl.ds(r0, c), :]
        v = vl_ref[pl.ds(r0, c), :]
        q_dec = (q * jnp.exp(b)).astype(BF16)
        k_inv = (k * jnp.exp(-b)).astype(BF16)
        vb = v.astype(BF16)
        stb = st.astype(BF16)
        keep = (ci <= ri) if d == 0 else (ci >= ri)
        outs = []
        for h in range(GLA_HEADS):
            ks = slice(GLA_DK * h, GLA_DK * (h + 1))
            vs = slice(GLA_DV * h, GLA_DV * (h + 1))
            sc = lax.dot_general(q_dec[:, ks], k_inv[:, ks], (((1,), (1,)), ((), ())), preferred_element_type=F32)
            sc = jnp.where(keep, sc, 0.0).astype(BF16)
            o_h = jnp.dot(sc, vb[:, vs], preferred_element_type=F32)
            o_h = o_h + lax.dot_general(q_dec[:, ks], stb[:, ks], (((1,), (1,)), ((), ())), preferred_element_type=F32)
            outs.append(o_h)
        o = jnp.concatenate(outs, axis=1)
        if first:
            o_ref[pl.ds(r0, c), :] = o
        else:
            o_ref[pl.ds(r0, c), :] = o_ref[pl.ds(r0, c), :] + o
        return state_update(st, k, v, b, b_last)

    n_lat = t_lat // c
    n_ctx = t_ctx // c
    st0 = jnp.zeros((GLA_DV, qk), F32)
    st = lax.fori_loop(0, n_ctx, lambda i, s: ctx_chunk(pl.multiple_of(i * c, c), 0, s), st0)
    lax.fori_loop(0, n_lat, lambda i, s: lat_chunk(pl.multiple_of(i * c, c), 0, s, True), st)
    st = lax.fori_loop(0, n_ctx, lambda i, s: ctx_chunk(pl.multiple_of((n_ctx - 1 - i) * c, c), 1, s), st0)
    lax.fori_loop(0, n_lat, lambda i, s: lat_chunk(pl.multiple_of((n_lat - 1 - i) * c, c), 1, s, False), st)


def _gla(h_l, h_c, wg_pad, bg, batch, t_lat, t_ctx):
    kern = functools.partial(_gla_kernel, t_lat=t_lat, t_ctx=t_ctx)
    qk = QK_WIDTH
    col = lambda t, width, off: pl.BlockSpec((t, width), lambda b: (b, off // width))
    return pl.pallas_call(
        kern,
        grid=(batch,),
        in_specs=[col(t_lat, qk, COL_Q), col(t_lat, qk, COL_K), col(t_lat, GLA_WIDTH, COL_V), col(t_lat, LANES, COL_LR),
                  col(t_ctx, qk, COL_K), col(t_ctx, GLA_WIDTH, COL_V), col(t_ctx, LANES, COL_LR),
                  pl.BlockSpec((2, LANES, qk), lambda b: (0, 0, 0)), pl.BlockSpec((2, 1, qk), lambda b: (0, 0, 0))],
        out_specs=pl.BlockSpec((t_lat, GLA_WIDTH), lambda b: (b, 0)),
        out_shape=jax.ShapeDtypeStruct((batch * t_lat, GLA_WIDTH), F32),
        compiler_params=_cparams(("arbitrary",)),
        name="gla",
    )(h_l, h_l, h_l, h_l, h_c, h_c, h_c, wg_pad, bg)


def _merge_kernel(y_ref, gate_ref, o_ref, r_ref, x_ref, pos_ref, mod_ref, ng_ref, w_ref, out_ref):
    rg = y_ref[...] * _gelu(gate_ref[...])
    og = o_ref[...]
    heads = [_rms(og[:, GLA_DV * h:GLA_DV * (h + 1)], ng_ref[...]) for h in range(GLA_HEADS)]
    gla = jnp.concatenate(heads, axis=1) * jax.nn.silu(r_ref[...])
    m = jnp.dot(rg.astype(BF16), w_ref[0:RG_WIDTH, :], preferred_element_type=F32)
    m = m + jnp.dot(gla.astype(BF16), w_ref[RG_WIDTH:, :], preferred_element_type=F32)
    out_ref[...] = x_ref[...] + pos_ref[...] + mod_ref[0] * m


def _merge(y_rg, h_l, o_gla, x2, pos, gate_mod, ng, w_out, seq_len):
    n, d = x2.shape
    tm = min(ROW_TILE, seq_len)
    per_seq = seq_len // tm
    w = RG_WIDTH
    return pl.pallas_call(
        _merge_kernel,
        grid=(n // tm,),
        in_specs=[pl.BlockSpec((tm, w), lambda i: (i, 0)),
                  pl.BlockSpec((tm, w), lambda i: (i, COL_RGATE // w)),
                  pl.BlockSpec((tm, GLA_WIDTH), lambda i: (i, 0)),
                  pl.BlockSpec((tm, GLA_WIDTH), lambda i: (i, COL_R // GLA_WIDTH)),
                  pl.BlockSpec((tm, d), lambda i: (i, 0)),
                  pl.BlockSpec((tm, d), lambda i: (i % per_seq, 0)),
                  pl.BlockSpec((1, 1, d), lambda i: (i // per_seq, 0, 0)),
                  pl.BlockSpec((1, GLA_DV), lambda i: (0, 0)),
                  pl.BlockSpec((w + GLA_WIDTH, d), lambda i: (0, 0))],
        out_specs=pl.BlockSpec((tm, d), lambda i: (i, 0)),
        out_shape=jax.ShapeDtypeStruct((n, d), F32),
        compiler_params=_cparams(("arbitrary",)),
        name="merge",
    )(y_rg, h_l, o_gla, h_l, x2, pos, gate_mod, ng, w_out)


def _sort_network(n):
    pairs = []
    p = 1
    while p < n:
        k = p
        while k >= 1:
            for j in range(k % p, n - k, 2 * k):
                for i in range(min(k, n - j - k)):
                    if (i + j) // (2 * p) == (i + j + k) // (2 * p):
                        pairs.append((i + j, i + j + k))
            k //= 2
        p *= 2
    return pairs


def _bitonic_network(n):
    pairs = []
    s = n // 2
    while s >= 1:
        pairs += [(i, i + s) for i in range(n) if (i & s) == 0]
        s //= 2
    return pairs


SORT16 = _sort_network(PEER_TOPK)
BITONIC16 = _bitonic_network(PEER_TOPK)
TAG_BITS = 14
PAD_TAG = 1 << 30


def _beats(a, b):
    return (a[0] > b[0]) | ((a[0] == b[0]) & (a[1] < b[1]))


def _exchange(w, i, j):
    a, b = w[i], w[j]
    c = _beats(a, b)
    w[i] = (jnp.maximum(a[0], b[0]), jnp.where(c, a[1], b[1]))
    w[j] = (jnp.minimum(a[0], b[0]), jnp.where(c, b[1], a[1]))


def _sorted16(w):
    w = list(w)
    for i, j in SORT16:
        _exchange(w, i, j)
    return w


def _merge_top16(a, b):
    w = []
    for i in range(PEER_TOPK):
        x, y = a[i], b[PEER_TOPK - 1 - i]
        c = _beats(x, y)
        w.append((jnp.maximum(x[0], y[0]), jnp.where(c, x[1], y[1])))
    for i, j in BITONIC16:
        _exchange(w, i, j)
    return w


def _top16_of(elems):
    lists = [_sorted16(elems[i:i + PEER_TOPK]) for i in range(0, len(elems), PEER_TOPK)]
    while len(lists) > 1:
        nxt = [_merge_top16(lists[i], lists[i + 1]) for i in range(0, len(lists) - 1, 2)]
        if len(lists) % 2:
            nxt.append(lists[-1])
        lists = nxt
    return lists[0]


def _product_key_top16(scores1, scores2):
    shape = scores1[0].shape
    tops = []
    for scores in (scores1, scores2):
        tops.append(_top16_of([(s, jnp.full(shape, k, jnp.int32)) for k, s in enumerate(scores)]))
    s1, s2 = tops
    cand = []
    for a in range(PEER_TOPK):
        for b in range(PEER_TOPK // (a + 1)):
            tag = (s1[a][1] * N_KEYS + s2[b][1]) + ((a * PEER_TOPK + b) << TAG_BITS)
            cand.append((s1[a][0] + s2[b][0], tag))
    pad = (jnp.full(shape, -jnp.inf, F32), jnp.full(shape, PAD_TAG, jnp.int32))
    cand += [pad] * ((-len(cand)) % PEER_TOPK)
    top = _top16_of(cand)
    ex = [jnp.exp(t[0] - top[0][0]) for t in top]
    tot = ex[0]
    for e in ex[1:]:
        tot = tot + e
    inv = 1.0 / tot
    return [t[1] & ((1 << TAG_BITS) - 1) for t in top], [e * inv for e in ex]


def _select_kernel(x_ref, shift_ref, scale_ref, g_ref, wq_ref, wk_ref, u_ref, e_ref, w_ref, s_scr):
    u = _rms(x_ref[...], g_ref[...]) * (1.0 + scale_ref[0]) + shift_ref[0]
    u_ref[...] = u
    q = jnp.dot(u.astype(BF16), wq_ref[...], preferred_element_type=F32).astype(BF16)
    hd = q.shape[1] // 2
    for p in range(2):
        s_scr[p] = lax.dot_general(wk_ref[p], q[:, p * hd:(p + 1) * hd], (((1,), (1,)), ((), ())),
                                   preferred_element_type=F32)
    h8 = PEER_HEADS

    def lane_group(gi, carry):
        lanes = pl.ds(pl.multiple_of(gi * LANES, LANES), LANES)
        slabs = [[s_scr[p, h8 * k:h8 * (k + 1), lanes] for k in range(N_KEYS)] for p in range(2)]
        experts, weights = _product_key_top16(slabs[0], slabs[1])
        for k in range(PEER_TOPK):
            e_ref[h8 * k:h8 * (k + 1), lanes] = experts[k]
            w_ref[h8 * k:h8 * (k + 1), lanes] = weights[k]
        return carry

    lax.fori_loop(0, x_ref.shape[0] // LANES, lane_group, 0)


def _select(x1, shift, scale, g, wq, wk, seq_len):
    n, d = x1.shape
    tq = min(SEL_TILE, seq_len)
    per_seq = seq_len // tq
    mod_map = lambda i: (i // per_seq, 0, 0)
    return pl.pallas_call(
        _select_kernel,
        grid=(n // tq,),
        in_specs=[pl.BlockSpec((tq, d), lambda i: (i, 0)),
                  pl.BlockSpec((1, 1, d), mod_map), pl.BlockSpec((1, 1, d), mod_map),
                  pl.BlockSpec((1, d), lambda i: (0, 0)),
                  pl.BlockSpec(wq.shape, lambda i: (0, 0)),
                  pl.BlockSpec(wk.shape, lambda i: (0, 0, 0))],
        out_specs=[pl.BlockSpec((tq, d), lambda i: (i, 0)),
                   pl.BlockSpec((N_SEL, tq), lambda i: (0, i)),
                   pl.BlockSpec((N_SEL, tq), lambda i: (0, i))],
        out_shape=[jax.ShapeDtypeStruct((n, d), F32),
                   jax.ShapeDtypeStruct((N_SEL, n), jnp.int32),
                   jax.ShapeDtypeStruct((N_SEL, n), F32)],
        scratch_shapes=[pltpu.VMEM((2, N_KEYS * PEER_HEADS, tq), F32)],
        compiler_params=_cparams(("arbitrary",)),
        name="peer_select",
    )(x1, shift, scale, g, wq, wk)


def _fold_sublanes(p):
    sub = lax.broadcasted_iota(jnp.int32, p.shape[2:], 0)
    half = SUBLANES // 2
    while half >= 1:
        k = p.shape[1] // 2
        a, b = p[:, :k], p[:, k:]
        low = (sub & half) == 0
        p = jnp.where(low, a, b) + jnp.where(low, pltpu.roll(a, SUBLANES - half, 2), pltpu.roll(b, half, 2))
        half //= 2
    return p[:, 0]


def _peer_kernel(idx0_ref, idxn_ref, u_ref, w_ref, x_ref, mod_ref, fg_ref, tab_hbm, out_ref, gbuf, xs, yacc, wsc, sem):
    tt = GATHER_TOKENS
    rows = tt * N_SEL
    d = x_ref.shape[1]
    s8 = SUBLANES
    groups = N_SEL // s8
    i = pl.program_id(0)
    n = pl.num_programs(0)

    def issue(idx_ref, slot, r_lo, r_hi):
        for r in range(r_lo, r_hi):
            e = idx_ref[0, 0, r]
            pltpu.make_async_copy(tab_hbm.at[pl.ds(e, 1)], gbuf.at[slot, pl.ds(r, 1)], sem.at[slot]).start(priority=r % 2)

    def wait_slot(slot):
        pltpu.make_async_copy(tab_hbm.at[pl.ds(0, rows)], gbuf.at[slot], sem.at[slot]).wait()

    @pl.when(i == 0)
    def _():
        issue(idx0_ref, 0, 0, rows)

    def step(cur, nxt):
        for t in range(tt):
            for s in range(s8):
                xs[t, s:s + 1, :] = u_ref[t:t + 1, LANES * s:LANES * (s + 1)]
        off = (i % (LANES // tt)) * tt
        wsc[...] = pltpu.roll(w_ref[...], (LANES - off) % LANES, 1)
        wait_slot(cur)
        half = s8 // 2
        for t in range(tt):
            xt = xs[t]
            dots = []
            for jb in range(groups):
                r0 = t * N_SEL + jb * s8
                issue(idxn_ref, nxt, r0, r0 + half)
                p = gbuf[cur, r0:r0 + s8, 0:s8, :] * xt
                dots.append(jnp.sum(_fold_sublanes(p[None]), axis=-1, keepdims=True)[0])
            wcol = wsc[:, t:t + 1]
            y = jnp.zeros((s8, LANES), F32)
            for jb in range(groups):
                r0 = t * N_SEL + jb * s8
                issue(idxn_ref, nxt, r0 + half, r0 + s8)
                act = _gelu(dots[jb]) * wcol[jb * s8:(jb + 1) * s8, :]
                v = gbuf[cur, r0:r0 + s8, s8:2 * s8, :]
                for jj in range(s8):
                    y = y + v[jj] * act[jj:jj + 1, :]
            for s in range(s8):
                yacc[t:t + 1, LANES * s:LANES * (s + 1)] = y[s:s + 1, :]

    @pl.when(i % 2 == 0)
    def _():
        step(0, 1)

    @pl.when(i % 2 == 1)
    def _():
        step(1, 0)

    @pl.when(i == n - 1)
    def _():
        wait_slot((i + 1) % 2)

    x2 = x_ref[...] + mod_ref[0] * yacc[...]
    out_ref[...] = _rms(x2, fg_ref[...])


def _peer(idx3, u, wts, x1, gate_mod, fg, uv_tab, seq_len):
    n, d = x1.shape
    tt = GATHER_TOKENS
    n_tiles = n // tt
    per_seq = seq_len // tt
    rows = tt * N_SEL
    smem_spec = lambda m: pl.BlockSpec((1, 1, rows), m, memory_space=pltpu.SMEM)
    return pl.pallas_call(
        _peer_kernel,
        grid=(n_tiles,),
        in_specs=[smem_spec(lambda i: (0, 0, 0)),
                  smem_spec(lambda i: (jnp.minimum(i + 1, n_tiles - 1), 0, 0)),
                  pl.BlockSpec((tt, d), lambda i: (i, 0)),
                  pl.BlockSpec((N_SEL, LANES), lambda i: (0, i // (LANES // tt))),
                  pl.BlockSpec((tt, d), lambda i: (i, 0)),
                  pl.BlockSpec((1, 1, d), lambda i: (i // per_seq, 0, 0)),
                  pl.BlockSpec((1, d), lambda i: (0, 0)),
                  pl.BlockSpec(memory_space=pl.ANY)],
        out_specs=pl.BlockSpec((tt, d), lambda i: (i, 0)),
        out_shape=jax.ShapeDtypeStruct((n, d), F32),
        scratch_shapes=[pltpu.VMEM((2, rows) + uv_tab.shape[1:], F32), pltpu.VMEM((tt, SUBLANES, LANES), F32),
                        pltpu.VMEM((tt, d), F32), pltpu.VMEM((N_SEL, LANES), F32), pltpu.SemaphoreType.DMA((2,))],
        compiler_params=_cparams(("arbitrary",)),
        name="peer_gather",
    )(idx3, idx3, u, wts, x1, gate_mod, fg, uv_tab)


def _block_diag(w):
    g, i, j = w.shape
    eye = jnp.eye(g, dtype=w.dtype)
    return (eye[:, None, :, None] * w[:, :, None, :]).reshape(g * i, g * j)


def kernel(x, c, ctx, c_ctx, ada_w, ada_b, norm1_g, w_in, conv_w, conv_b, rg_w_a, rg_b_a, rg_w_x, rg_b_x, rg_lambda, gla_w_g, gla_b_g, gla_norm_g, w_out, norm2_g, peer_w_q, peer_keys, peer_u, peer_v, final_norm_g):
    bsz, t_lat, d = x.shape
    t_ctx = ctx.shape[1]
    assert ada_w.shape[0] == 1, "single-layer block"
    n = bsz * t_lat
    x2 = x.reshape(n, d)
    ctx2 = ctx.reshape(bsz * t_ctx, d)
    pos = _sincos_2d(t_lat, d)

    pad_rows = (-(bsz + 1)) % SUBLANES
    cc = jnp.concatenate([c, c_ctx[None, :], jnp.zeros((pad_rows, d), F32)], axis=0)
    mod = _adaln(cc, ada_w[0], ada_b[0][None, :])
    mod_l = mod[:bsz].reshape(bsz, N_MOD, 1, d)
    mod_c = mod[bsz].reshape(N_MOD, 1, 1, d)

    w_in_p = jnp.pad(w_in[0], ((0, 0), (0, IN_COLS_PAD - w_in.shape[2]))).astype(BF16)
    g1 = norm1_g[0][None, :]
    h_l = _inproj(x2, pos, mod_l[:, 0], mod_l[:, 1], g1, w_in_p, t_lat)
    h_c = _inproj(ctx2, None, mod_c[0], mod_c[1], g1, w_in_p, t_ctx)

    wg_rg = jnp.concatenate([_block_diag(rg_w_a[0, 0]), _block_diag(rg_w_x[0, 0]),
                             _block_diag(rg_w_a[0, 1]), _block_diag(rg_w_x[0, 1])], axis=1).astype(BF16)
    bg_rg = jnp.concatenate([rg_b_a[0, 0], rg_b_x[0, 0], rg_b_a[0, 1], rg_b_x[0, 1]])[None, :]
    y_rg = _rglru(h_l, h_c, conv_w[0], conv_b[0][None, :], wg_rg, bg_rg, rg_lambda[0], bsz, t_lat, t_ctx)

    wg_gla = jnp.zeros((2, LANES, QK_WIDTH), F32)
    wg_gla = wg_gla.at[0, 0:GLA_RANK].set(gla_w_g[0, 0]).at[1, GLA_RANK:2 * GLA_RANK].set(gla_w_g[0, 1])
    o_gla = _gla(h_l, h_c, wg_gla, gla_b_g[0][:, None, :], bsz, t_lat, t_ctx)

    x1 = _merge(y_rg, h_l, o_gla, x2, pos, mod_l[:, 2], gla_norm_g[0][None, :], w_out[0].astype(BF16), t_lat)

    half = PEER_DQ // 2
    wq = peer_w_q[0].reshape(d, PEER_HEADS, 2, half).transpose(0, 2, 1, 3).reshape(d, 2 * PEER_HEADS * half).astype(BF16)
    kt = peer_keys[0].transpose(1, 2, 0, 3)
    wk = (kt[:, :, :, None, :] * jnp.eye(PEER_HEADS, dtype=F32)[None, None, :, :, None])
    wk = wk.reshape(2, N_KEYS * PEER_HEADS, PEER_HEADS * half).astype(BF16)
    u, eidx, wts = _select(x1, mod_l[:, 3], mod_l[:, 4], norm2_g[0][None, :], wq, wk, t_lat)
    tt = GATHER_TOKENS
    idx3 = eidx.T.reshape(n // tt, 1, tt * N_SEL)
    uv_tab = jnp.concatenate([peer_u[0], peer_v[0]], axis=1).reshape(peer_u.shape[1], 2 * d // LANES, LANES)
    out = _peer(idx3, u, wts, x1, mod_l[:, 5], final_norm_g[None, :], uv_tab, t_lat)
    return out.reshape(bsz, t_lat, d)
```

```python
import functools
import math

import jax
import jax.numpy as jnp
from jax import lax
from jax.experimental import pallas as pl
from jax.experimental.pallas import tpu as pltpu

F32 = jnp.float32
BF16 = jnp.bfloat16

GRID_W = 64
POS_THETA = 10000.0
RMS_EPS = 1e-6
N_MOD = 6
RG_WIDTH = 512
RG_BLOCKS = 8
RG_C = 8.0
CONV_W = 4
GLA_HEADS = 4
GLA_DK = 64
GLA_DV = 128
GLA_WIDTH = GLA_HEADS * GLA_DV
GLA_RANK = 16
GLA_GATE_NORM = 16.0
GLA_CHUNK = 64
QK_WIDTH = GLA_HEADS * GLA_DK
N_KEYS = 128
PEER_HEADS = 8
PEER_DQ = 256
PEER_TOPK = 16
N_SEL = PEER_HEADS * PEER_TOPK

LANES = 128
SUBLANES = 8
VMEM_LIMIT = 56 * 1024 * 1024

COL_RX, COL_RGATE, COL_Q, COL_K, COL_V, COL_R, COL_LR = 0, 512, 1024, 1280, 1536, 2048, 2560
IN_COLS_PAD = 2688

SCAN_BLOCK = 128
HALO = 8
ROW_TILE = 512
SEL_TILE = 256
GATHER_TOKENS = 8


def _cparams(sem, vmem=VMEM_LIMIT):
    return pltpu.CompilerParams(dimension_semantics=sem, vmem_limit_bytes=vmem)


def _sincos_2d(n_tokens, dim):
    rows = n_tokens // GRID_W
    r, col = jnp.meshgrid(jnp.arange(rows, dtype=F32), jnp.arange(GRID_W, dtype=F32), indexing="ij")
    quarter = dim // 4
    omega = POS_THETA ** (-jnp.arange(quarter, dtype=F32) / quarter)

    def axis_embed(p):
        ang = p.reshape(-1)[:, None] * omega[None, :]
        return jnp.concatenate([jnp.sin(ang), jnp.cos(ang)], axis=-1)

    return jnp.concatenate([axis_embed(r), axis_embed(col)], axis=-1)


def _gelu(x):
    return 0.5 * x * (1.0 + lax.erf(x * (1.0 / math.sqrt(2.0))))


def _rms(x, g):
    ms = jnp.mean(x * x, axis=-1, keepdims=True)
    return x * lax.rsqrt(ms + RMS_EPS) * g


def _adaln_kernel(c_ref, w_ref, b_ref, o_ref):
    a = jax.nn.silu(c_ref[...]).astype(BF16)
    o_ref[...] = jnp.dot(a, w_ref[...].astype(BF16), preferred_element_type=F32) + b_ref[...]


def _adaln(cc, w, b):
    rows, d = cc.shape
    n = w.shape[1]
    tn = 1536
    return pl.pallas_call(
        _adaln_kernel,
        grid=(n // tn,),
        in_specs=[pl.BlockSpec((rows, d), lambda j: (0, 0)),
                  pl.BlockSpec((d, tn), lambda j: (0, j)),
                  pl.BlockSpec((1, tn), lambda j: (0, j))],
        out_specs=pl.BlockSpec((rows, tn), lambda j: (0, j)),
        out_shape=jax.ShapeDtypeStruct((rows, n), F32),
        compiler_params=_cparams(("arbitrary",)),
        name="adaln",
    )(cc, w, b)


def _inproj_kernel(*refs, has_pos):
    if has_pos:
        x_ref, pos_ref, shift_ref, scale_ref, g_ref, w_ref, o_ref = refs
        x = x_ref[...] + pos_ref[...]
    else:
        x_ref, shift_ref, scale_ref, g_ref, w_ref, o_ref = refs
        x = x_ref[...]
    y = _rms(x, g_ref[...]) * (1.0 + scale_ref[0]) + shift_ref[0]
    o_ref[...] = jnp.dot(y.astype(BF16), w_ref[...], preferred_element_type=F32)


def _inproj(x2, pos, shift, scale, g, w, seq_len):
    n, d = x2.shape
    cols = w.shape[1]
    tm = min(ROW_TILE, seq_len)
    per_seq = seq_len // tm
    shared = shift.shape[0] == 1
    mod_map = (lambda i: (0, 0, 0)) if shared else (lambda i: (i // per_seq, 0, 0))
    in_specs = [pl.BlockSpec((tm, d), lambda i: (i, 0))]
    args = [x2]
    if pos is not None:
        in_specs.append(pl.BlockSpec((tm, d), lambda i: (i % per_seq, 0)))
        args.append(pos)
    in_specs += [pl.BlockSpec((1, 1, d), mod_map), pl.BlockSpec((1, 1, d), mod_map),
                 pl.BlockSpec((1, d), lambda i: (0, 0)), pl.BlockSpec((d, cols), lambda i: (0, 0))]
    args += [shift, scale, g, w]
    return pl.pallas_call(
        functools.partial(_inproj_kernel, has_pos=pos is not None),
        grid=(n // tm,),
        in_specs=in_specs,
        out_specs=pl.BlockSpec((tm, cols), lambda i: (i, 0)),
        out_shape=jax.ShapeDtypeStruct((n, cols), F32),
        compiler_params=_cparams(("arbitrary",)),
        name="inproj",
    )(*args)


def _rglru_kernel(rxl_ref, rxc_ref, cw_ref, cb_ref, wg_ref, bg_ref, lam_ref, y_ref, xpl_ref, xpc_ref, *, t_lat, t_ctx):
    tb = SCAN_BLOCK
    w = RG_WIDTH
    zeros_halo = jnp.zeros((HALO, w), F32)
    for xp_ref, src_ref, t in ((xpl_ref, rxl_ref, t_lat), (xpc_ref, rxc_ref, t_ctx)):
        xp_ref[0:HALO, :] = zeros_halo
        xp_ref[HALO:HALO + t, :] = src_ref[...]
        xp_ref[HALO + t:HALO + t + HALO, :] = zeros_halo

    cw = cw_ref[...]
    cb = cb_ref[...]
    row = lax.broadcasted_iota(jnp.int32, (tb, w), 0)

    def block(xp_ref, r0, d, h_in):
        ext = xp_ref[pl.ds(r0, tb + 2 * HALO), :]
        xc = (ext[HALO - 2:HALO - 2 + tb] * cw[0:1] + ext[HALO - 1:HALO - 1 + tb] * cw[1:2]
              + ext[HALO:HALO + tb] * cw[2:3] + ext[HALO + 1:HALO + 1 + tb] * cw[3:4] + cb)
        g = jnp.dot(xc.astype(BF16), wg_ref[:, 2 * w * d:2 * w * (d + 1)], preferred_element_type=F32)
        g = g + bg_ref[:, 2 * w * d:2 * w * (d + 1)]
        gate_r = jax.nn.sigmoid(g[:, :w])
        gate_i = jax.nn.sigmoid(g[:, w:])
        log_a = (-RG_C * jax.nn.softplus(-lam_ref[d:d + 1, :])) * gate_r
        a = jnp.exp(log_a)
        th = jnp.tanh(log_a)
        bv = jnp.sqrt(-2.0 * th / (1.0 - th)) * gate_i * xc
        s = 1
        while s < tb:
            if d == 0:
                valid = row >= s
                a_p = jnp.where(valid, pltpu.roll(a, s, 0), 1.0)
                b_p = jnp.where(valid, pltpu.roll(bv, s, 0), 0.0)
            else:
                valid = row < tb - s
                a_p = jnp.where(valid, pltpu.roll(a, tb - s, 0), 1.0)
                b_p = jnp.where(valid, pltpu.roll(bv, tb - s, 0), 0.0)
            bv = a * b_p + bv
            a = a * a_p
            s *= 2
        h = a * h_in + bv
        h_out = h[tb - 1:tb] if d == 0 else h[0:1]
        return h, h_out

    n_lat = t_lat // tb
    n_ctx = t_ctx // tb
    h0 = jnp.zeros((1, w), F32)

    def ctx_f(i, h):
        return block(xpc_ref, pl.multiple_of(i * tb, tb), 0, h)[1]

    def lat_f(i, h):
        r0 = pl.multiple_of(i * tb, tb)
        hb, h_out = block(xpl_ref, r0, 0, h)
        y_ref[pl.ds(r0, tb), :] = hb
        return h_out

    h = lax.fori_loop(0, n_ctx, ctx_f, h0)
    lax.fori_loop(0, n_lat, lat_f, h)

    def ctx_b(i, h):
        return block(xpc_ref, pl.multiple_of((n_ctx - 1 - i) * tb, tb), 1, h)[1]

    def lat_b(i, h):
        r0 = pl.multiple_of((n_lat - 1 - i) * tb, tb)
        hb, h_out = block(xpl_ref, r0, 1, h)
        y_ref[pl.ds(r0, tb), :] = y_ref[pl.ds(r0, tb), :] + hb
        return h_out

    h = lax.fori_loop(0, n_ctx, ctx_b, h0)
    lax.fori_loop(0, n_lat, lat_b, h)


def _rglru(h_l, h_c, conv_w, conv_b, wg, bg, lam, batch, t_lat, t_ctx):
    w = RG_WIDTH
    kern = functools.partial(_rglru_kernel, t_lat=t_lat, t_ctx=t_ctx)
    full = lambda shape: pl.BlockSpec(shape, lambda b: tuple(0 for _ in shape))
    return pl.pallas_call(
        kern,
        grid=(batch,),
        in_specs=[pl.BlockSpec((t_lat, w), lambda b: (b, COL_RX // w)),
                  pl.BlockSpec((t_ctx, w), lambda b: (b, COL_RX // w)),
                  full((CONV_W, w)), full((1, w)), full((w, 4 * w)), full((1, 4 * w)), full((2, w))],
        out_specs=pl.BlockSpec((t_lat, w), lambda b: (b, 0)),
        out_shape=jax.ShapeDtypeStruct((batch * t_lat, w), F32),
        scratch_shapes=[pltpu.VMEM((t_lat + 2 * HALO, w), F32), pltpu.VMEM((t_ctx + 2 * HALO, w), F32)],
        compiler_params=_cparams(("arbitrary",)),
        name="rglru",
    )(h_l, h_c, conv_w, conv_b, wg, bg, lam)


def _gla_kernel(ql_ref, kl_ref, vl_ref, lrl_ref, kc_ref, vc_ref, lrc_ref, wg_ref, bg_ref, o_ref, *, t_lat, t_ctx):
    c = GLA_CHUNK
    qk = QK_WIDTH
    scale = GLA_DK ** -0.5
    row = lax.broadcasted_iota(jnp.int32, (c, qk), 0)
    ri = lax.broadcasted_iota(jnp.int32, (c, c), 0)
    ci = lax.broadcasted_iota(jnp.int32, (c, c), 1)

    def decays(lr_ref, r0, d):
        z = jnp.dot(lr_ref[pl.ds(r0, c), :].astype(BF16), wg_ref[d].astype(BF16), preferred_element_type=F32) + bg_ref[d]
        b = jax.nn.log_sigmoid(z) * (1.0 / GLA_GATE_NORM)
        s = 1
        while s < c:
            if d == 0:
                b = b + jnp.where(row >= s, pltpu.roll(b, s, 0), 0.0)
            else:
                b = b + jnp.where(row < c - s, pltpu.roll(b, c - s, 0), 0.0)
            s *= 2
        b_last = b[c - 1:c] if d == 0 else b[0:1]
        return b, b_last

    def state_update(st, k, v, b, b_last):
        k_dec = (k * jnp.exp(b_last - b)).astype(BF16)
        vb = v.astype(BF16)
        parts = []
        for h in range(GLA_HEADS):
            parts.append(lax.dot_general(vb[:, GLA_DV * h:GLA_DV * (h + 1)], k_dec[:, GLA_DK * h:GLA_DK * (h + 1)],
                                         (((0,), (0,)), ((), ())), preferred_element_type=F32))
        return st * jnp.exp(b_last) + jnp.concatenate(parts, axis=1)

    def ctx_chunk(r0, d, st):
        b, b_last = decays(lrc_ref, r0, d)
        return state_update(st, kc_ref[pl.ds(r0, c), :], vc_ref[pl.ds(r0, c), :], b, b_last)

    def lat_chunk(r0, d, st, first):
        b, b_last = decays(lrl_ref, r0, d)
        q = ql_ref[pl.ds(r0, c), :] * scale
        k = kl_ref[pl.ds(r0, c), :]
        v = vl_ref[pl.ds(r0, c), :]
        q_dec = (q * jnp.exp(b)).astype(BF16)
        k_inv = (k * jnp.exp(-b)).astype(BF16)
        vb = v.astype(BF16)
        stb = st.astype(BF16)
        keep = (ci <= ri) if d == 0 else (ci >= ri)
        outs = []
        for h in range(GLA_HEADS):
            ks = slice(GLA_DK * h, GLA_DK * (h + 1))
            vs = slice(GLA_DV * h, GLA_DV * (h + 1))
            sc = lax.dot_general(q_dec[:, ks], k_inv[:, ks], (((1,), (1,)), ((), ())), preferred_element_type=F32)
            sc = jnp.where(keep, sc, 0.0).astype(BF16)
            o_h = jnp.dot(sc, vb[:, vs], preferred_element_type=F32)
            o_h = o_h + lax.dot_general(q_dec[:, ks], stb[:, ks], (((1,), (1,)), ((), ())), preferred_element_type=F32)
            outs.append(o_h)
        o = jnp.concatenate(outs, axis=1)
        if first:
            o_ref[pl.ds(r0, c), :] = o
        else:
            o_ref[pl.ds(r0, c), :] = o_ref[pl.ds(r0, c), :] + o
        return state_update(st, k, v, b, b_last)

    n_lat = t_lat // c
    n_ctx = t_ctx // c
    st0 = jnp.zeros((GLA_DV, qk), F32)
    st = lax.fori_loop(0, n_ctx, lambda i, s: ctx_chunk(pl.multiple_of(i * c, c), 0, s), st0)
    lax.fori_loop(0, n_lat, lambda i, s: lat_chunk(pl.multiple_of(i * c, c), 0, s, True), st)
    st = lax.fori_loop(0, n_ctx, lambda i, s: ctx_chunk(pl.multiple_of((n_ctx - 1 - i) * c, c), 1, s), st0)
    lax.fori_loop(0, n_lat, lambda i, s: lat_chunk(pl.multiple_of((n_lat - 1 - i) * c, c), 1, s, False), st)


def _gla(h_l, h_c, wg_pad, bg, batch, t_lat, t_ctx):
    kern = functools.partial(_gla_kernel, t_lat=t_lat, t_ctx=t_ctx)
    qk = QK_WIDTH
    col = lambda t, width, off: pl.BlockSpec((t, width), lambda b: (b, off // width))
    return pl.pallas_call(
        kern,
        grid=(batch,),
        in_specs=[col(t_lat, qk, COL_Q), col(t_lat, qk, COL_K), col(t_lat, GLA_WIDTH, COL_V), col(t_lat, LANES, COL_LR),
                  col(t_ctx, qk, COL_K), col(t_ctx, GLA_WIDTH, COL_V), col(t_ctx, LANES, COL_LR),
                  pl.BlockSpec((2, LANES, qk), lambda b: (0, 0, 0)), pl.BlockSpec((2, 1, qk), lambda b: (0, 0, 0))],
        out_specs=pl.BlockSpec((t_lat, GLA_WIDTH), lambda b: (b, 0)),
        out_shape=jax.ShapeDtypeStruct((batch * t_lat, GLA_WIDTH), F32),
        compiler_params=_cparams(("arbitrary",)),
        name="gla",
    )(h_l, h_l, h_l, h_l, h_c, h_c, h_c, wg_pad, bg)


def _merge_kernel(y_ref, gate_ref, o_ref, r_ref, x_ref, pos_ref, mod_ref, ng_ref, w_ref, out_ref):
    rg = y_ref[...] * _gelu(gate_ref[...])
    og = o_ref[...]
    heads = [_rms(og[:, GLA_DV * h:GLA_DV * (h + 1)], ng_ref[...]) for h in range(GLA_HEADS)]
    gla = jnp.concatenate(heads, axis=1) * jax.nn.silu(r_ref[...])
    m = jnp.dot(rg.astype(BF16), w_ref[0:RG_WIDTH, :], preferred_element_type=F32)
    m = m + jnp.dot(gla.astype(BF16), w_ref[RG_WIDTH:, :], preferred_element_type=F32)
    out_ref[...] = x_ref[...] + pos_ref[...] + mod_ref[0] * m


def _merge(y_rg, h_l, o_gla, x2, pos, gate_mod, ng, w_out, seq_len):
    n, d = x2.shape
    tm = min(ROW_TILE, seq_len)
    per_seq = seq_len // tm
    w = RG_WIDTH
    return pl.pallas_call(
        _merge_kernel,
        grid=(n // tm,),
        in_specs=[pl.BlockSpec((tm, w), lambda i: (i, 0)),
                  pl.BlockSpec((tm, w), lambda i: (i, COL_RGATE // w)),
                  pl.BlockSpec((tm, GLA_WIDTH), lambda i: (i, 0)),
                  pl.BlockSpec((tm, GLA_WIDTH), lambda i: (i, COL_R // GLA_WIDTH)),
                  pl.BlockSpec((tm, d), lambda i: (i, 0)),
                  pl.BlockSpec((tm, d), lambda i: (i % per_seq, 0)),
                  pl.BlockSpec((1, 1, d), lambda i: (i // per_seq, 0, 0)),
                  pl.BlockSpec((1, GLA_DV), lambda i: (0, 0)),
                  pl.BlockSpec((w + GLA_WIDTH, d), lambda i: (0, 0))],
        out_specs=pl.BlockSpec((tm, d), lambda i: (i, 0)),
        out_shape=jax.ShapeDtypeStruct((n, d), F32),
        compiler_params=_cparams(("arbitrary",)),
        name="merge",
    )(y_rg, h_l, o_gla, h_l, x2, pos, gate_mod, ng, w_out)


def _sort_network(n):
    pairs = []
    p = 1
    while p < n:
        k = p
        while k >= 1:
            for j in range(k % p, n - k, 2 * k):
                for i in range(min(k, n - j - k)):
                    if (i + j) // (2 * p) == (i + j + k) // (2 * p):
                        pairs.append((i + j, i + j + k))
            k //= 2
        p *= 2
    return pairs


def _bitonic_network(n):
    pairs = []
    s = n // 2
    while s >= 1:
        pairs += [(i, i + s) for i in range(n) if (i & s) == 0]
        s //= 2
    return pairs


SORT16 = _sort_network(PEER_TOPK)
BITONIC16 = _bitonic_network(PEER_TOPK)
TAG_BITS = 14
PAD_TAG = 1 << 30


def _beats(a, b):
    return (a[0] > b[0]) | ((a[0] == b[0]) & (a[1] < b[1]))


def _exchange(w, i, j):
    a, b = w[i], w[j]
    c = _beats(a, b)
    w[i] = (jnp.maximum(a[0], b[0]), jnp.where(c, a[1], b[1]))
    w[j] = (jnp.minimum(a[0], b[0]), jnp.where(c, b[1], a[1]))


def _sorted16(w):
    w = list(w)
    for i, j in SORT16:
        _exchange(w, i, j)
    return w


def _merge_top16(a, b):
    w = []
    for i in range(PEER_TOPK):
        x, y = a[i], b[PEER_TOPK - 1 - i]
        c = _beats(x, y)
        w.append((jnp.maximum(x[0], y[0]), jnp.where(c, x[1], y[1])))
    for i, j in BITONIC16:
        _exchange(w, i, j)
    return w


def _top16_of(elems):
    lists = [_sorted16(elems[i:i + PEER_TOPK]) for i in range(0, len(elems), PEER_TOPK)]
    while len(lists) > 1:
        nxt = [_merge_top16(lists[i], lists[i + 1]) for i in range(0, len(lists) - 1, 2)]
        if len(lists) % 2:
            nxt.append(lists[-1])
        lists = nxt
    return lists[0]


def _product_key_top16(scores1, scores2):
    shape = scores1[0].shape
    tops = []
    for scores in (scores1, scores2):
        tops.append(_top16_of([(s, jnp.full(shape, k, jnp.int32)) for k, s in enumerate(scores)]))
    s1, s2 = tops
    cand = []
    for a in range(PEER_TOPK):
        for b in range(PEER_TOPK // (a + 1)):
            tag = (s1[a][1] * N_KEYS + s2[b][1]) + ((a * PEER_TOPK + b) << TAG_BITS)
            cand.append((s1[a][0] + s2[b][0], tag))
    pad = (jnp.full(shape, -jnp.inf, F32), jnp.full(shape, PAD_TAG, jnp.int32))
    cand += [pad] * ((-len(cand)) % PEER_TOPK)
    top = _top16_of(cand)
    ex = [jnp.exp(t[0] - top[0][0]) for t in top]
    tot = ex[0]
    for e in ex[1:]:
        tot = tot + e
    inv = 1.0 / tot
    return [t[1] & ((1 << TAG_BITS) - 1) for t in top], [e * inv for e in ex]


def _select_kernel(x_ref, shift_ref, scale_ref, g_ref, wq_ref, wk_ref, u_ref, e_ref, w_ref, s_scr):
    u = _rms(x_ref[...], g_ref[...]) * (1.0 + scale_ref[0]) + shift_ref[0]
    u_ref[...] = u
    q = jnp.dot(u.astype(BF16), wq_ref[...], preferred_element_type=F32).astype(BF16)
    hd = q.shape[1] // 2
    for p in range(2):
        s_scr[p] = lax.dot_general(wk_ref[p], q[:, p * hd:(p + 1) * hd], (((1,), (1,)), ((), ())),
                                   preferred_element_type=F32)
    h8 = PEER_HEADS

    def lane_group(gi, carry):
        lanes = pl.ds(pl.multiple_of(gi * LANES, LANES), LANES)
        slabs = [[s_scr[p, h8 * k:h8 * (k + 1), lanes] for k in range(N_KEYS)] for p in range(2)]
        experts, weights = _product_key_top16(slabs[0], slabs[1])
        for k in range(PEER_TOPK):
            e_ref[h8 * k:h8 * (k + 1), lanes] = experts[k]
            w_ref[h8 * k:h8 * (k + 1), lanes] = weights[k]
        return carry

    lax.fori_loop(0, x_ref.shape[0] // LANES, lane_group, 0)


def _select(x1, shift, scale, g, wq, wk, seq_len):
    n, d = x1.shape
    tq = min(SEL_TILE, seq_len)
    per_seq = seq_len // tq
    mod_map = lambda i: (i // per_seq, 0, 0)
    return pl.pallas_call(
        _select_kernel,
        grid=(n // tq,),
        in_specs=[pl.BlockSpec((tq, d), lambda i: (i, 0)),
                  pl.BlockSpec((1, 1, d), mod_map), pl.BlockSpec((1, 1, d), mod_map),
                  pl.BlockSpec((1, d), lambda i: (0, 0)),
                  pl.BlockSpec(wq.shape, lambda i: (0, 0)),
                  pl.BlockSpec(wk.shape, lambda i: (0, 0, 0))],
        out_specs=[pl.BlockSpec((tq, d), lambda i: (i, 0)),
                   pl.BlockSpec((N_SEL, tq), lambda i: (0, i)),
                   pl.BlockSpec((N_SEL, tq), lambda i: (0, i))],
        out_shape=[jax.ShapeDtypeStruct((n, d), F32),
                   jax.ShapeDtypeStruct((N_SEL, n), jnp.int32),
                   jax.ShapeDtypeStruct((N_SEL, n), F32)],
        scratch_shapes=[pltpu.VMEM((2, N_KEYS * PEER_HEADS, tq), F32)],
        compiler_params=_cparams(("arbitrary",)),
        name="peer_select",
    )(x1, shift, scale, g, wq, wk)


def _fold_sublanes(p):
    sub = lax.broadcasted_iota(jnp.int32, p.shape[2:], 0)
    half = SUBLANES // 2
    while half >= 1:
        k = p.shape[1] // 2
        a, b = p[:, :k], p[:, k:]
        low = (sub & half) == 0
        p = jnp.where(low, a, b) + jnp.where(low, pltpu.roll(a, SUBLANES - half, 2), pltpu.roll(b, half, 2))
        half //= 2
    return p[:, 0]


def _peer_kernel(idx0_ref, idxn_ref, u_ref, w_ref, x_ref, mod_ref, fg_ref, tab_hbm, out_ref, gbuf, xs, yacc, wsc, sem):
    tt = GATHER_TOKENS
    rows = tt * N_SEL
    d = x_ref.shape[1]
    s8 = SUBLANES
    groups = N_SEL // s8
    i = pl.program_id(0)
    n = pl.num_programs(0)

    def issue(idx_ref, slot, r_lo, r_hi):
        for r in range(r_lo, r_hi):
            e = idx_ref[0, 0, r]
            pltpu.make_async_copy(tab_hbm.at[pl.ds(e, 1)], gbuf.at[slot, pl.ds(r, 1)], sem.at[slot]).start(priority=r % 2)

    def wait_slot(slot):
        pltpu.make_async_copy(tab_hbm.at[pl.ds(0, rows)], gbuf.at[slot], sem.at[slot]).wait()

    @pl.when(i == 0)
    def _():
        issue(idx0_ref, 0, 0, rows)

    def step(cur, nxt):
        for t in range(tt):
            for s in range(s8):
                xs[t, s:s + 1, :] = u_ref[t:t + 1, LANES * s:LANES * (s + 1)]
        off = (i % (LANES // tt)) * tt
        wsc[...] = pltpu.roll(w_ref[...], (LANES - off) % LANES, 1)
        wait_slot(cur)
        half = s8 // 2
        for t in range(tt):
            xt = xs[t]
            dots = []
            for jb in range(groups):
                r0 = t * N_SEL + jb * s8
                issue(idxn_ref, nxt, r0, r0 + half)
                p = gbuf[cur, r0:r0 + s8, 0:s8, :].astype(F32) * xt
                dots.append(jnp.sum(_fold_sublanes(p[None]), axis=-1, keepdims=True)[0])
            wcol = wsc[:, t:t + 1]
            y = jnp.zeros((s8, LANES), F32)
            for jb in range(groups):
                r0 = t * N_SEL + jb * s8
                issue(idxn_ref, nxt, r0 + half, r0 + s8)
                act = _gelu(dots[jb]) * wcol[jb * s8:(jb + 1) * s8, :]
                v = gbuf[cur, r0:r0 + s8, s8:2 * s8, :].astype(F32)
                for jj in range(s8):
                    y = y + v[jj] * act[jj:jj + 1, :]
            for s in range(s8):
                yacc[t:t + 1, LANES * s:LANES * (s + 1)] = y[s:s + 1, :]

    @pl.when(i % 2 == 0)
    def _():
        step(0, 1)

    @pl.when(i % 2 == 1)
    def _():
        step(1, 0)

    @pl.when(i == n - 1)
    def _():
        wait_slot((i + 1) % 2)

    x2 = x_ref[...] + mod_ref[0] * yacc[...]
    out_ref[...] = _rms(x2, fg_ref[...])


def _peer(idx3, u, wts, x1, gate_mod, fg, uv_tab, seq_len, t0, n):
    d = x1.shape[1]
    tt = GATHER_TOKENS
    n_tiles = n // tt
    off = t0 // tt
    per_seq = seq_len // tt
    rows = tt * N_SEL
    smem_spec = lambda m: pl.BlockSpec((1, 1, rows), m, memory_space=pltpu.SMEM)
    return pl.pallas_call(
        _peer_kernel,
        grid=(n_tiles,),
        in_specs=[smem_spec(lambda i: (off, 0, 0)),
                  smem_spec(lambda i: (off + jnp.minimum(i + 1, n_tiles - 1), 0, 0)),
                  pl.BlockSpec((tt, d), lambda i: (off + i, 0)),
                  pl.BlockSpec((N_SEL, LANES), lambda i: (0, (off + i) // (LANES // tt))),
                  pl.BlockSpec((tt, d), lambda i: (off + i, 0)),
                  pl.BlockSpec((1, 1, d), lambda i: ((off + i) // per_seq, 0, 0)),
                  pl.BlockSpec((1, d), lambda i: (0, 0)),
                  pl.BlockSpec(memory_space=pl.ANY)],
        out_specs=pl.BlockSpec((tt, d), lambda i: (i, 0)),
        out_shape=jax.ShapeDtypeStruct((n, d), F32),
        scratch_shapes=[pltpu.VMEM((2, rows) + uv_tab.shape[1:], uv_tab.dtype), pltpu.VMEM((tt, SUBLANES, LANES), F32),
                        pltpu.VMEM((tt, d), F32), pltpu.VMEM((N_SEL, LANES), F32), pltpu.SemaphoreType.DMA((2,))],
        compiler_params=_cparams(("arbitrary",)),
        name="peer_gather",
    )(idx3, idx3, u, wts, x1, gate_mod, fg, uv_tab)


SC_LANES = 16
SC_CORES = 2
SC_SUBCORES = 16
SC_WORKERS = SC_CORES * SC_SUBCORES
SC_CHUNK = 32
SC_BLOCK = 4
SC_SEQUENCES = 5
TC_FIRST_SEQUENCES = 6


def _sc_params():
    import dataclasses
    cp = pltpu.CompilerParams()
    if "needs_layout_passes" in pltpu.CompilerParams.__dataclass_fields__:
        cp = dataclasses.replace(cp, needs_layout_passes=False)
    return cp


def _sc_mesh():
    from jax.experimental.pallas import tpu_sc as plsc
    return plsc.VectorSubcoreMesh(core_axis_name="c", subcore_axis_name="s", num_cores=SC_CORES, num_subcores=SC_SUBCORES)


def _sc_gather_pipeline(tab_hbm, idx_v, bufs, sems, n_chunks, compute):
    def copy(h, b):
        return pltpu.make_async_copy(tab_hbm.at[idx_v.at[pl.ds(h * SC_CHUNK, SC_CHUNK)]], bufs[b], sems[b])

    copy(0, 0).start()

    def pair(hp, carry):
        h0 = 2 * hp
        copy(h0 + 1, 1).start()
        copy(h0, 0).wait()
        compute(h0, bufs[0])

        @pl.when(hp + 1 < n_chunks // 2)
        def _():
            copy(h0 + 2, 0).start()
        copy(h0 + 1, 1).wait()
        compute(h0 + 1, bufs[1])
        return carry

    lax.fori_loop(0, n_chunks // 2, pair, 0)


def _sc_dots(u, idx_tm, tab, t0, n_tok):
    from jax.experimental.pallas import tpu_sc as plsc
    d = u.shape[1]
    n_sel = idx_tm.shape[1]
    per_w = n_tok // SC_WORKERS
    ln, rb_n, ch = SC_LANES, SC_BLOCK, SC_CHUNK
    tile = tab.shape[1:]
    cps = LANES // ln

    def body(u_hbm, idx_hbm, tab_hbm, out_hbm, x_v, idx_v, rows0, rows1, d_v, acc_v, sem0, sem1):
        wid = lax.axis_index("s") * SC_CORES + lax.axis_index("c")
        lane = lax.iota(jnp.int32, ln)

        def token(t, carry):
            loc = wid * per_w + t
            pltpu.sync_copy(u_hbm.at[t0 + loc], x_v)
            pltpu.sync_copy(idx_hbm.at[t0 + loc], idx_v)

            def compute(h, rows_v):
                def group(g, c2):
                    def quad(rb, c3):
                        accs = [jnp.zeros((ln,), F32) for _ in range(rb_n)]
                        for c in range(d // ln):
                            xv = x_v[pl.ds(c * ln, ln)]
                            for k in range(rb_n):
                                accs[k] = accs[k] + rows_v[g * ln + rb * rb_n + k, c // cps, pl.ds((c % cps) * ln, ln)] * xv
                        for k in range(rb_n):
                            acc_v[rb * rb_n + k, :] = accs[k]
                        return c3
                    lax.fori_loop(0, ln // rb_n, quad, 0)
                    dvec = jnp.zeros((ln,), F32)
                    for l in range(ln):
                        dvec = dvec + plsc.load_gather(acc_v, [lane, jnp.full((ln,), l, jnp.int32)])
                    d_v[pl.ds(h * ch + g * ln, ln)] = dvec
                    return c2
                lax.fori_loop(0, ch // ln, group, 0)

            _sc_gather_pipeline(tab_hbm, idx_v, (rows0, rows1), (sem0, sem1), n_sel // ch, compute)
            pltpu.sync_copy(d_v, out_hbm.at[loc])
            return carry

        lax.fori_loop(0, per_w, token, 0)

    return pl.kernel(
        body, out_type=jax.ShapeDtypeStruct((n_tok, n_sel), F32), mesh=_sc_mesh(),
        scratch_types=[pltpu.VMEM((d,), F32), pltpu.VMEM((n_sel,), jnp.int32),
                       pltpu.VMEM((ch,) + tile, F32), pltpu.VMEM((ch,) + tile, F32),
                       pltpu.VMEM((n_sel,), F32), pltpu.VMEM((ln, ln), F32),
                       pltpu.SemaphoreType.DMA, pltpu.SemaphoreType.DMA],
        compiler_params=_sc_params(), name="sc_dots")(u, idx_tm, tab)


def _sc_mix(act, idx_tm, tab, t0, n_tok):
    tile = tab.shape[1:]
    d = tile[0] * tile[1]
    n_sel = idx_tm.shape[1]
    per_w = n_tok // SC_WORKERS
    ln, cb_n, ch = SC_LANES, SC_BLOCK, SC_CHUNK
    bps = LANES // (cb_n * ln)

    def body(act_hbm, idx_hbm, tab_hbm, out_hbm, a_v, idx_v, rows0, rows1, sp_v, y_v, sem0, sem1):
        wid = lax.axis_index("s") * SC_CORES + lax.axis_index("c")
        lane = lax.iota(jnp.int32, ln)

        def token(t, carry):
            loc = wid * per_w + t
            pltpu.sync_copy(act_hbm.at[loc], a_v)
            pltpu.sync_copy(idx_hbm.at[t0 + loc], idx_v)

            def splat(g, c2):
                avec = a_v[pl.ds(g * ln, ln)]
                for r in range(ln):
                    sp_v[g * ln + r, :] = jnp.full((ln,), jnp.sum(jnp.where(lane == r, avec, 0.0)), F32)
                return c2
            lax.fori_loop(0, n_sel // ln, splat, 0)

            def zero(c, c2):
                y_v[pl.ds(c * ln, ln)] = jnp.zeros((ln,), F32)
                return c2
            lax.fori_loop(0, d // ln, zero, 0)

            def compute(h, rows_v):
                def colblock(cb, c2):
                    base = cb * (cb_n * ln)
                    accs = [y_v[pl.ds(base + k * ln, ln)] for k in range(cb_n)]
                    for r in range(ch):
                        ar = sp_v[h * ch + r, :]
                        for k in range(cb_n):
                            accs[k] = accs[k] + rows_v[r, cb // bps, pl.ds((cb % bps) * (cb_n * ln) + k * ln, ln)] * ar
                    for k in range(cb_n):
                        y_v[pl.ds(base + k * ln, ln)] = accs[k]
                    return c2
                lax.fori_loop(0, d // (cb_n * ln), colblock, 0)

            _sc_gather_pipeline(tab_hbm, idx_v, (rows0, rows1), (sem0, sem1), n_sel // ch, compute)
            pltpu.sync_copy(y_v, out_hbm.at[loc])
            return carry

        lax.fori_loop(0, per_w, token, 0)

    return pl.kernel(
        body, out_type=jax.ShapeDtypeStruct((n_tok, d), F32), mesh=_sc_mesh(),
        scratch_types=[pltpu.VMEM((n_sel,), F32), pltpu.VMEM((n_sel,), jnp.int32),
                       pltpu.VMEM((ch,) + tile, F32), pltpu.VMEM((ch,) + tile, F32),
                       pltpu.VMEM((n_sel, ln), F32), pltpu.VMEM((d,), F32),
                       pltpu.SemaphoreType.DMA, pltpu.SemaphoreType.DMA],
        compiler_params=_sc_params(), name="sc_mix")(act, idx_tm, tab)


def _act_kernel(d_ref, w_ref, after_ref, o_ref):
    del after_ref
    o_ref[...] = _gelu(d_ref[...]) * w_ref[...]


def _activation(dots, w_tm, after):
    n, k = dots.shape
    tm = min(ROW_TILE * 4, n)
    spec = pl.BlockSpec((tm, k), lambda i: (i, 0))
    anchor = pl.BlockSpec((SUBLANES, after.shape[1]), lambda i: (0, 0))
    return pl.pallas_call(_act_kernel, grid=(n // tm,), in_specs=[spec, spec, anchor], out_specs=spec,
                          out_shape=jax.ShapeDtypeStruct((n, k), F32),
                          compiler_params=_cparams(("arbitrary",)), name="peer_act")(dots, w_tm, after)


def _finish_kernel(x_ref, y_ref, mod_ref, fg_ref, o_ref):
    o_ref[...] = _rms(x_ref[...] + mod_ref[0] * y_ref[...], fg_ref[...])


def _finish(x1, y, gate_mod, fg, seq_len, t0):
    n, d = y.shape
    tm = min(ROW_TILE, seq_len)
    per_seq = seq_len // tm
    first = t0 // tm
    return pl.pallas_call(
        _finish_kernel,
        grid=(n // tm,),
        in_specs=[pl.BlockSpec((tm, d), lambda i: (first + i, 0)),
                  pl.BlockSpec((tm, d), lambda i: (i, 0)),
                  pl.BlockSpec((1, 1, d), lambda i: ((first + i) // per_seq, 0, 0)),
                  pl.BlockSpec((1, d), lambda i: (0, 0))],
        out_specs=pl.BlockSpec((tm, d), lambda i: (i, 0)),
        out_shape=jax.ShapeDtypeStruct((n, d), F32),
        compiler_params=_cparams(("arbitrary",)),
        name="peer_finish",
    )(x1, y, gate_mod, fg)


def _block_diag(w):
    g, i, j = w.shape
    eye = jnp.eye(g, dtype=w.dtype)
    return (eye[:, None, :, None] * w[:, :, None, :]).reshape(g * i, g * j)


def kernel(x, c, ctx, c_ctx, ada_w, ada_b, norm1_g, w_in, conv_w, conv_b, rg_w_a, rg_b_a, rg_w_x, rg_b_x, rg_lambda, gla_w_g, gla_b_g, gla_norm_g, w_out, norm2_g, peer_w_q, peer_keys, peer_u, peer_v, final_norm_g):
    bsz, t_lat, d = x.shape
    t_ctx = ctx.shape[1]
    assert ada_w.shape[0] == 1, "single-layer block"
    n = bsz * t_lat
    x2 = x.reshape(n, d)
    ctx2 = ctx.reshape(bsz * t_ctx, d)
    pos = _sincos_2d(t_lat, d)

    pad_rows = (-(bsz + 1)) % SUBLANES
    cc = jnp.concatenate([c, c_ctx[None, :], jnp.zeros((pad_rows, d), F32)], axis=0)
    mod = _adaln(cc, ada_w[0], ada_b[0][None, :])
    mod_l = mod[:bsz].reshape(bsz, N_MOD, 1, d)
    mod_c = mod[bsz].reshape(N_MOD, 1, 1, d)

    w_in_p = jnp.pad(w_in[0], ((0, 0), (0, IN_COLS_PAD - w_in.shape[2]))).astype(BF16)
    g1 = norm1_g[0][None, :]
    h_l = _inproj(x2, pos, mod_l[:, 0], mod_l[:, 1], g1, w_in_p, t_lat)
    h_c = _inproj(ctx2, None, mod_c[0], mod_c[1], g1, w_in_p, t_ctx)

    wg_rg = jnp.concatenate([_block_diag(rg_w_a[0, 0]), _block_diag(rg_w_x[0, 0]),
                             _block_diag(rg_w_a[0, 1]), _block_diag(rg_w_x[0, 1])], axis=1).astype(BF16)
    bg_rg = jnp.concatenate([rg_b_a[0, 0], rg_b_x[0, 0], rg_b_a[0, 1], rg_b_x[0, 1]])[None, :]
    y_rg = _rglru(h_l, h_c, conv_w[0], conv_b[0][None, :], wg_rg, bg_rg, rg_lambda[0], bsz, t_lat, t_ctx)

    wg_gla = jnp.zeros((2, LANES, QK_WIDTH), F32)
    wg_gla = wg_gla.at[0, 0:GLA_RANK].set(gla_w_g[0, 0]).at[1, GLA_RANK:2 * GLA_RANK].set(gla_w_g[0, 1])
    o_gla = _gla(h_l, h_c, wg_gla, gla_b_g[0][:, None, :], bsz, t_lat, t_ctx)

    x1 = _merge(y_rg, h_l, o_gla, x2, pos, mod_l[:, 2], gla_norm_g[0][None, :], w_out[0].astype(BF16), t_lat)

    half = PEER_DQ // 2
    wq = peer_w_q[0].reshape(d, PEER_HEADS, 2, half).transpose(0, 2, 1, 3).reshape(d, 2 * PEER_HEADS * half).astype(BF16)
    kt = peer_keys[0].transpose(1, 2, 0, 3)
    wk = (kt[:, :, :, None, :] * jnp.eye(PEER_HEADS, dtype=F32)[None, None, :, :, None])
    wk = wk.reshape(2, N_KEYS * PEER_HEADS, PEER_HEADS * half).astype(BF16)
    u, eidx, wts = _select(x1, mod_l[:, 3], mod_l[:, 4], norm2_g[0][None, :], wq, wk, t_lat)
    tt = GATHER_TOKENS
    idx_tm = eidx.T
    idx3 = idx_tm.reshape(n // tt, 1, tt * N_SEL)
    u_tiles = peer_u[0].reshape(peer_u.shape[1], d // LANES, LANES)
    v_tiles = peer_v[0].reshape(peer_v.shape[1], d // LANES, LANES)
    uv_tab = jnp.concatenate([u_tiles, v_tiles], axis=1).astype(BF16)
    fg = final_norm_g[None, :]
    sc_seqs = SC_SEQUENCES if (bsz > SC_SEQUENCES and t_lat % (4 * ROW_TILE) == 0) else 0
    n_tc = (bsz - sc_seqs) * t_lat
    gate = mod_l[:, 5]
    if sc_seqs:
        n_a = min(TC_FIRST_SEQUENCES, bsz - sc_seqs - 1) * t_lat
        n_sc = n - n_tc
        out_a = _peer(idx3, u, wts, x1, gate, fg, uv_tab, t_lat, 0, n_a)
        dots = _sc_dots(u, idx_tm, u_tiles, n_tc, n_sc)
        act = _activation(dots, wts[:, n_tc:].T, out_a)
        out_b = _peer(idx3, u, wts, x1, gate, fg, uv_tab, t_lat, n_a, n_tc - n_a)
        y_sc = _sc_mix(act, idx_tm, v_tiles, n_tc, n_sc)
        out = jnp.concatenate([out_a, out_b, _finish(x1, y_sc, gate, fg, t_lat, n_tc)], axis=0)
    else:
        out = _peer(idx3, u, wts, x1, gate, fg, uv_tab, t_lat, 0, n)
    return out.reshape(bsz, t_lat, d)
```

```python
import functools
import math

import jax
import jax.numpy as jnp
from jax import lax
from jax.experimental import pallas as pl
from jax.experimental.pallas import tpu as pltpu

F32 = jnp.float32
BF16 = jnp.bfloat16

GRID_W = 64
POS_THETA = 10000.0
RMS_EPS = 1e-6
N_MOD = 6
RG_WIDTH = 512
RG_BLOCKS = 8
RG_C = 8.0
CONV_W = 4
GLA_HEADS = 4
GLA_DK = 64
GLA_DV = 128
GLA_WIDTH = GLA_HEADS * GLA_DV
GLA_RANK = 16
GLA_GATE_NORM = 16.0
GLA_CHUNK = 64
QK_WIDTH = GLA_HEADS * GLA_DK
N_KEYS = 128
PEER_HEADS = 8
PEER_DQ = 256
PEER_TOPK = 16
N_SEL = PEER_HEADS * PEER_TOPK

LANES = 128
SUBLANES = 8
VMEM_LIMIT = 56 * 1024 * 1024

COL_RX, COL_RGATE, COL_Q, COL_K, COL_V, COL_R, COL_LR = 0, 512, 1024, 1280, 1536, 2048, 2560
IN_COLS_PAD = 2688

SCAN_BLOCK = 128
HALO = 8
ROW_TILE = 512
SEL_TILE = 256
GATHER_TOKENS = 8


def _cparams(sem, vmem=VMEM_LIMIT):
    return pltpu.CompilerParams(dimension_semantics=sem, vmem_limit_bytes=vmem)


def _sincos_2d(n_tokens, dim):
    rows = n_tokens // GRID_W
    r, col = jnp.meshgrid(jnp.arange(rows, dtype=F32), jnp.arange(GRID_W, dtype=F32), indexing="ij")
    quarter = dim // 4
    omega = POS_THETA ** (-jnp.arange(quarter, dtype=F32) / quarter)

    def axis_embed(p):
        ang = p.reshape(-1)[:, None] * omega[None, :]
        return jnp.concatenate([jnp.sin(ang), jnp.cos(ang)], axis=-1)

    return jnp.concatenate([axis_embed(r), axis_embed(col)], axis=-1)


def _gelu(x):
    return 0.5 * x * (1.0 + lax.erf(x * (1.0 / math.sqrt(2.0))))


def _rms(x, g):
    ms = jnp.mean(x * x, axis=-1, keepdims=True)
    return x * lax.rsqrt(ms + RMS_EPS) * g


def _adaln_kernel(c_ref, w_ref, b_ref, o_ref):
    a = jax.nn.silu(c_ref[...]).astype(BF16)
    o_ref[...] = jnp.dot(a, w_ref[...].astype(BF16), preferred_element_type=F32) + b_ref[...]


def _adaln(cc, w, b):
    rows, d = cc.shape
    n = w.shape[1]
    tn = 1536
    return pl.pallas_call(
        _adaln_kernel,
        grid=(n // tn,),
        in_specs=[pl.BlockSpec((rows, d), lambda j: (0, 0)),
                  pl.BlockSpec((d, tn), lambda j: (0, j)),
                  pl.BlockSpec((1, tn), lambda j: (0, j))],
        out_specs=pl.BlockSpec((rows, tn), lambda j: (0, j)),
        out_shape=jax.ShapeDtypeStruct((rows, n), F32),
        compiler_params=_cparams(("arbitrary",)),
        name="adaln",
    )(cc, w, b)


def _inproj_kernel(*refs, has_pos):
    if has_pos:
        x_ref, pos_ref, shift_ref, scale_ref, g_ref, w_ref, o_ref = refs
        x = x_ref[...] + pos_ref[...]
    else:
        x_ref, shift_ref, scale_ref, g_ref, w_ref, o_ref = refs
        x = x_ref[...]
    y = _rms(x, g_ref[...]) * (1.0 + scale_ref[0]) + shift_ref[0]
    o_ref[...] = jnp.dot(y.astype(BF16), w_ref[...], preferred_element_type=F32)


def _inproj(x2, pos, shift, scale, g, w, seq_len):
    n, d = x2.shape
    cols = w.shape[1]
    tm = min(ROW_TILE, seq_len)
    per_seq = seq_len // tm
    shared = shift.shape[0] == 1
    mod_map = (lambda i: (0, 0, 0)) if shared else (lambda i: (i // per_seq, 0, 0))
    in_specs = [pl.BlockSpec((tm, d), lambda i: (i, 0))]
    args = [x2]
    if pos is not None:
        in_specs.append(pl.BlockSpec((tm, d), lambda i: (i % per_seq, 0)))
        args.append(pos)
    in_specs += [pl.BlockSpec((1, 1, d), mod_map), pl.BlockSpec((1, 1, d), mod_map),
                 pl.BlockSpec((1, d), lambda i: (0, 0)), pl.BlockSpec((d, cols), lambda i: (0, 0))]
    args += [shift, scale, g, w]
    return pl.pallas_call(
        functools.partial(_inproj_kernel, has_pos=pos is not None),
        grid=(n // tm,),
        in_specs=in_specs,
        out_specs=pl.BlockSpec((tm, cols), lambda i: (i, 0)),
        out_shape=jax.ShapeDtypeStruct((n, cols), F32),
        compiler_params=_cparams(("arbitrary",)),
        name="inproj",
    )(*args)


def _rglru_kernel(rxl_ref, rxc_ref, cw_ref, cb_ref, wg_ref, bg_ref, lam_ref, y_ref, xpl_ref, xpc_ref, *, t_lat, t_ctx):
    tb = SCAN_BLOCK
    w = RG_WIDTH
    zeros_halo = jnp.zeros((HALO, w), F32)
    for xp_ref, src_ref, t in ((xpl_ref, rxl_ref, t_lat), (xpc_ref, rxc_ref, t_ctx)):
        xp_ref[0:HALO, :] = zeros_halo
        xp_ref[HALO:HALO + t, :] = src_ref[...]
        xp_ref[HALO + t:HALO + t + HALO, :] = zeros_halo

    cw = cw_ref[...]
    cb = cb_ref[...]
    row = lax.broadcasted_iota(jnp.int32, (tb, w), 0)

    def block(xp_ref, r0, d, h_in):
        ext = xp_ref[pl.ds(r0, tb + 2 * HALO), :]
        xc = (ext[HALO - 2:HALO - 2 + tb] * cw[0:1] + ext[HALO - 1:HALO - 1 + tb] * cw[1:2]
              + ext[HALO:HALO + tb] * cw[2:3] + ext[HALO + 1:HALO + 1 + tb] * cw[3:4] + cb)
        g = jnp.dot(xc.astype(BF16), wg_ref[:, 2 * w * d:2 * w * (d + 1)], preferred_element_type=F32)
        g = g + bg_ref[:, 2 * w * d:2 * w * (d + 1)]
        gate_r = jax.nn.sigmoid(g[:, :w])
        gate_i = jax.nn.sigmoid(g[:, w:])
        log_a = (-RG_C * jax.nn.softplus(-lam_ref[d:d + 1, :])) * gate_r
        a = jnp.exp(log_a)
        th = jnp.tanh(log_a)
        bv = jnp.sqrt(-2.0 * th / (1.0 - th)) * gate_i * xc
        s = 1
        while s < tb:
            if d == 0:
                valid = row >= s
                a_p = jnp.where(valid, pltpu.roll(a, s, 0), 1.0)
                b_p = jnp.where(valid, pltpu.roll(bv, s, 0), 0.0)
            else:
                valid = row < tb - s
                a_p = jnp.where(valid, pltpu.roll(a, tb - s, 0), 1.0)
                b_p = jnp.where(valid, pltpu.roll(bv, tb - s, 0), 0.0)
            bv = a * b_p + bv
            a = a * a_p
            s *= 2
        h = a * h_in + bv
        h_out = h[tb - 1:tb] if d == 0 else h[0:1]
        return h, h_out

    n_lat = t_lat // tb
    n_ctx = t_ctx // tb
    h0 = jnp.zeros((1, w), F32)

    y_ref[...] = jnp.zeros(y_ref.shape, F32)

    def both_ctx(i, hs):
        return (block(xpc_ref, pl.multiple_of(i * tb, tb), 0, hs[0])[1],
                block(xpc_ref, pl.multiple_of((n_ctx - 1 - i) * tb, tb), 1, hs[1])[1])

    def both_lat(i, hs):
        rf = pl.multiple_of(i * tb, tb)
        hf, hf_out = block(xpl_ref, rf, 0, hs[0])
        y_ref[pl.ds(rf, tb), :] = y_ref[pl.ds(rf, tb), :] + hf
        rb = pl.multiple_of((n_lat - 1 - i) * tb, tb)
        hb, hb_out = block(xpl_ref, rb, 1, hs[1])
        y_ref[pl.ds(rb, tb), :] = y_ref[pl.ds(rb, tb), :] + hb
        return hf_out, hb_out

    lax.fori_loop(0, n_lat, both_lat, lax.fori_loop(0, n_ctx, both_ctx, (h0, h0)))


def _rglru(h_l, h_c, conv_w, conv_b, wg, bg, lam, batch, t_lat, t_ctx):
    w = RG_WIDTH
    kern = functools.partial(_rglru_kernel, t_lat=t_lat, t_ctx=t_ctx)
    full = lambda shape: pl.BlockSpec(shape, lambda b: tuple(0 for _ in shape))
    return pl.pallas_call(
        kern,
        grid=(batch,),
        in_specs=[pl.BlockSpec((t_lat, w), lambda b: (b, COL_RX // w)),
                  pl.BlockSpec((t_ctx, w), lambda b: (b, COL_RX // w)),
                  full((CONV_W, w)), full((1, w)), full((w, 4 * w)), full((1, 4 * w)), full((2, w))],
        out_specs=pl.BlockSpec((t_lat, w), lambda b: (b, 0)),
        out_shape=jax.ShapeDtypeStruct((batch * t_lat, w), F32),
        scratch_shapes=[pltpu.VMEM((t_lat + 2 * HALO, w), F32), pltpu.VMEM((t_ctx + 2 * HALO, w), F32)],
        compiler_params=_cparams(("arbitrary",)),
        name="rglru",
    )(h_l, h_c, conv_w, conv_b, wg, bg, lam)


def _gla_kernel(ql_ref, kl_ref, vl_ref, lrl_ref, kc_ref, vc_ref, lrc_ref, wg_ref, bg_ref, o_ref, *, t_lat, t_ctx):
    c = GLA_CHUNK
    qk = QK_WIDTH
    scale = GLA_DK ** -0.5
    row = lax.broadcasted_iota(jnp.int32, (c, qk), 0)
    ri = lax.broadcasted_iota(jnp.int32, (c, c), 0)
    ci = lax.broadcasted_iota(jnp.int32, (c, c), 1)

    def decays(lr_ref, r0, d):
        z = jnp.dot(lr_ref[pl.ds(r0, c), :].astype(BF16), wg_ref[d].astype(BF16), preferred_element_type=F32) + bg_ref[d]
        b = jax.nn.log_sigmoid(z) * (1.0 / GLA_GATE_NORM)
        s = 1
        while s < c:
            if d == 0:
                b = b + jnp.where(row >= s, pltpu.roll(b, s, 0), 0.0)
            else:
                b = b + jnp.where(row < c - s, pltpu.roll(b, c - s, 0), 0.0)
            s *= 2
        b_last = b[c - 1:c] if d == 0 else b[0:1]
        return b, b_last

    def state_update(st, k, v, b, b_last):
        k_dec = (k * jnp.exp(b_last - b)).astype(BF16)
        vb = v.astype(BF16)
        parts = []
        for h in range(GLA_HEADS):
            parts.append(lax.dot_general(vb[:, GLA_DV * h:GLA_DV * (h + 1)], k_dec[:, GLA_DK * h:GLA_DK * (h + 1)],
                                         (((0,), (0,)), ((), ())), preferred_element_type=F32))
        return st * jnp.exp(b_last) + jnp.concatenate(parts, axis=1)

    def ctx_chunk(r0, d, st):
        b, b_last = decays(lrc_ref, r0, d)
        return state_update(st, kc_ref[pl.ds(r0, c), :], vc_ref[pl.ds(r0, c), :], b, b_last)

    def lat_chunk(r0, d, st):
        b, b_last = decays(lrl_ref, r0, d)
        q = ql_ref[pl.ds(r0, c), :] * scale
        k = kl_ref[pl.ds(r0, c), :]
        v = vl_ref[pl.ds(r0, c), :]
        q_dec = (q * jnp.exp(b)).astype(BF16)
        k_inv = (k * jnp.exp(-b)).astype(BF16)
        vb = v.astype(BF16)
        stb = st.astype(BF16)
        keep = (ci <= ri) if d == 0 else (ci >= ri)
        outs = []
        for h in range(GLA_HEADS):
            ks = slice(GLA_DK * h, GLA_DK * (h + 1))
            vs = slice(GLA_DV * h, GLA_DV * (h + 1))
            sc = lax.dot_general(q_dec[:, ks], k_inv[:, ks], (((1,), (1,)), ((), ())), preferred_element_type=F32)
            sc = jnp.where(keep, sc, 0.0).astype(BF16)
            o_h = jnp.dot(sc, vb[:, vs], preferred_element_type=F32)
            o_h = o_h + lax.dot_general(q_dec[:, ks], stb[:, ks], (((1,), (1,)), ((), ())), preferred_element_type=F32)
            outs.append(o_h)
        o = jnp.concatenate(outs, axis=1)
        o_ref[pl.ds(r0, c), :] = o_ref[pl.ds(r0, c), :] + o
        return state_update(st, k, v, b, b_last)

    n_lat = t_lat // c
    n_ctx = t_ctx // c
    st0 = jnp.zeros((GLA_DV, qk), F32)
    o_ref[...] = jnp.zeros(o_ref.shape, F32)

    def both_ctx(i, carry):
        return (ctx_chunk(pl.multiple_of(i * c, c), 0, carry[0]),
                ctx_chunk(pl.multiple_of((n_ctx - 1 - i) * c, c), 1, carry[1]))

    def both_lat(i, carry):
        return (lat_chunk(pl.multiple_of(i * c, c), 0, carry[0]),
                lat_chunk(pl.multiple_of((n_lat - 1 - i) * c, c), 1, carry[1]))

    lax.fori_loop(0, n_lat, both_lat, lax.fori_loop(0, n_ctx, both_ctx, (st0, st0)))


def _gla(h_l, h_c, wg_pad, bg, batch, t_lat, t_ctx):
    kern = functools.partial(_gla_kernel, t_lat=t_lat, t_ctx=t_ctx)
    qk = QK_WIDTH
    col = lambda t, width, off: pl.BlockSpec((t, width), lambda b: (b, off // width))
    return pl.pallas_call(
        kern,
        grid=(batch,),
        in_specs=[col(t_lat, qk, COL_Q), col(t_lat, qk, COL_K), col(t_lat, GLA_WIDTH, COL_V), col(t_lat, LANES, COL_LR),
                  col(t_ctx, qk, COL_K), col(t_ctx, GLA_WIDTH, COL_V), col(t_ctx, LANES, COL_LR),
                  pl.BlockSpec((2, LANES, qk), lambda b: (0, 0, 0)), pl.BlockSpec((2, 1, qk), lambda b: (0, 0, 0))],
        out_specs=pl.BlockSpec((t_lat, GLA_WIDTH), lambda b: (b, 0)),
        out_shape=jax.ShapeDtypeStruct((batch * t_lat, GLA_WIDTH), F32),
        compiler_params=_cparams(("arbitrary",)),
        name="gla",
    )(h_l, h_l, h_l, h_l, h_c, h_c, h_c, wg_pad, bg)


def _merge_kernel(y_ref, gate_ref, o_ref, r_ref, x_ref, pos_ref, mod_ref, ng_ref, w_ref, out_ref):
    rg = y_ref[...] * _gelu(gate_ref[...])
    og = o_ref[...]
    heads = [_rms(og[:, GLA_DV * h:GLA_DV * (h + 1)], ng_ref[...]) for h in range(GLA_HEADS)]
    gla = jnp.concatenate(heads, axis=1) * jax.nn.silu(r_ref[...])
    m = jnp.dot(rg.astype(BF16), w_ref[0:RG_WIDTH, :], preferred_element_type=F32)
    m = m + jnp.dot(gla.astype(BF16), w_ref[RG_WIDTH:, :], preferred_element_type=F32)
    out_ref[...] = x_ref[...] + pos_ref[...] + mod_ref[0] * m


def _merge(y_rg, h_l, o_gla, x2, pos, gate_mod, ng, w_out, seq_len):
    n, d = x2.shape
    tm = min(ROW_TILE, seq_len)
    per_seq = seq_len // tm
    w = RG_WIDTH
    return pl.pallas_call(
        _merge_kernel,
        grid=(n // tm,),
        in_specs=[pl.BlockSpec((tm, w), lambda i: (i, 0)),
                  pl.BlockSpec((tm, w), lambda i: (i, COL_RGATE // w)),
                  pl.BlockSpec((tm, GLA_WIDTH), lambda i: (i, 0)),
                  pl.BlockSpec((tm, GLA_WIDTH), lambda i: (i, COL_R // GLA_WIDTH)),
                  pl.BlockSpec((tm, d), lambda i: (i, 0)),
                  pl.BlockSpec((tm, d), lambda i: (i % per_seq, 0)),
                  pl.BlockSpec((1, 1, d), lambda i: (i // per_seq, 0, 0)),
                  pl.BlockSpec((1, GLA_DV), lambda i: (0, 0)),
                  pl.BlockSpec((w + GLA_WIDTH, d), lambda i: (0, 0))],
        out_specs=pl.BlockSpec((tm, d), lambda i: (i, 0)),
        out_shape=jax.ShapeDtypeStruct((n, d), F32),
        compiler_params=_cparams(("arbitrary",)),
        name="merge",
    )(y_rg, h_l, o_gla, h_l, x2, pos, gate_mod, ng, w_out)


def _sort_network(n):
    pairs = []
    p = 1
    while p < n:
        k = p
        while k >= 1:
            for j in range(k % p, n - k, 2 * k):
                for i in range(min(k, n - j - k)):
                    if (i + j) // (2 * p) == (i + j + k) // (2 * p):
                        pairs.append((i + j, i + j + k))
            k //= 2
        p *= 2
    return pairs


def _bitonic_network(n):
    pairs = []
    s = n // 2
    while s >= 1:
        pairs += [(i, i + s) for i in range(n) if (i & s) == 0]
        s //= 2
    return pairs


SORT16 = _sort_network(PEER_TOPK)
BITONIC16 = _bitonic_network(PEER_TOPK)
TAG_BITS = 14
PAD_TAG = 1 << 30


def _beats(a, b):
    return (a[0] > b[0]) | ((a[0] == b[0]) & (a[1] < b[1]))


def _exchange(w, i, j):
    a, b = w[i], w[j]
    c = _beats(a, b)
    w[i] = (jnp.maximum(a[0], b[0]), jnp.where(c, a[1], b[1]))
    w[j] = (jnp.minimum(a[0], b[0]), jnp.where(c, b[1], a[1]))


def _sorted16(w):
    w = list(w)
    for i, j in SORT16:
        _exchange(w, i, j)
    return w


def _merge_top16(a, b):
    w = []
    for i in range(PEER_TOPK):
        x, y = a[i], b[PEER_TOPK - 1 - i]
        c = _beats(x, y)
        w.append((jnp.maximum(x[0], y[0]), jnp.where(c, x[1], y[1])))
    for i, j in BITONIC16:
        _exchange(w, i, j)
    return w


def _top16_of(elems):
    lists = [_sorted16(elems[i:i + PEER_TOPK]) for i in range(0, len(elems), PEER_TOPK)]
    while len(lists) > 1:
        nxt = [_merge_top16(lists[i], lists[i + 1]) for i in range(0, len(lists) - 1, 2)]
        if len(lists) % 2:
            nxt.append(lists[-1])
        lists = nxt
    return lists[0]


def _product_key_top16(scores1, scores2):
    shape = scores1[0].shape
    tops = []
    for scores in (scores1, scores2):
        tops.append(_top16_of([(s, jnp.full(shape, k, jnp.int32)) for k, s in enumerate(scores)]))
    s1, s2 = tops
    cand = []
    for a in range(PEER_TOPK):
        for b in range(PEER_TOPK // (a + 1)):
            tag = (s1[a][1] * N_KEYS + s2[b][1]) + ((a * PEER_TOPK + b) << TAG_BITS)
            cand.append((s1[a][0] + s2[b][0], tag))
    pad = (jnp.full(shape, -jnp.inf, F32), jnp.full(shape, PAD_TAG, jnp.int32))
    cand += [pad] * ((-len(cand)) % PEER_TOPK)
    top = _top16_of(cand)
    ex = [jnp.exp(t[0] - top[0][0]) for t in top]
    tot = ex[0]
    for e in ex[1:]:
        tot = tot + e
    inv = 1.0 / tot
    return [t[1] & ((1 << TAG_BITS) - 1) for t in top], [e * inv for e in ex]


def _select_kernel(x_ref, shift_ref, scale_ref, g_ref, wq_ref, wk_ref, u_ref, e_ref, w_ref, s_scr):
    u = _rms(x_ref[...], g_ref[...]) * (1.0 + scale_ref[0]) + shift_ref[0]
    u_ref[...] = u
    q = jnp.dot(u.astype(BF16), wq_ref[...], preferred_element_type=F32).astype(BF16)
    hd = q.shape[1] // 2
    for p in range(2):
        s_scr[p] = lax.dot_general(wk_ref[p], q[:, p * hd:(p + 1) * hd], (((1,), (1,)), ((), ())),
                                   preferred_element_type=F32)
    h8 = PEER_HEADS

    def lane_group(gi, carry):
        lanes = pl.ds(pl.multiple_of(gi * LANES, LANES), LANES)
        slabs = [[s_scr[p, h8 * k:h8 * (k + 1), lanes] for k in range(N_KEYS)] for p in range(2)]
        experts, weights = _product_key_top16(slabs[0], slabs[1])
        for k in range(PEER_TOPK):
            e_ref[h8 * k:h8 * (k + 1), lanes] = experts[k]
            w_ref[h8 * k:h8 * (k + 1), lanes] = weights[k]
        return carry

    lax.fori_loop(0, x_ref.shape[0] // LANES, lane_group, 0)


def _select(x1, shift, scale, g, wq, wk, seq_len):
    n, d = x1.shape
    tq = min(SEL_TILE, seq_len)
    per_seq = seq_len // tq
    mod_map = lambda i: (i // per_seq, 0, 0)
    return pl.pallas_call(
        _select_kernel,
        grid=(n // tq,),
        in_specs=[pl.BlockSpec((tq, d), lambda i: (i, 0)),
                  pl.BlockSpec((1, 1, d), mod_map), pl.BlockSpec((1, 1, d), mod_map),
                  pl.BlockSpec((1, d), lambda i: (0, 0)),
                  pl.BlockSpec(wq.shape, lambda i: (0, 0)),
                  pl.BlockSpec(wk.shape, lambda i: (0, 0, 0))],
        out_specs=[pl.BlockSpec((tq, d), lambda i: (i, 0)),
                   pl.BlockSpec((N_SEL, tq), lambda i: (0, i)),
                   pl.BlockSpec((N_SEL, tq), lambda i: (0, i))],
        out_shape=[jax.ShapeDtypeStruct((n, d), F32),
                   jax.ShapeDtypeStruct((N_SEL, n), jnp.int32),
                   jax.ShapeDtypeStruct((N_SEL, n), F32)],
        scratch_shapes=[pltpu.VMEM((2, N_KEYS * PEER_HEADS, tq), F32)],
        compiler_params=_cparams(("arbitrary",)),
        name="peer_select",
    )(x1, shift, scale, g, wq, wk)


def _fold_sublanes(p):
    sub = lax.broadcasted_iota(jnp.int32, p.shape[2:], 0)
    half = SUBLANES // 2
    while half >= 1:
        k = p.shape[1] // 2
        a, b = p[:, :k], p[:, k:]
        low = (sub & half) == 0
        p = jnp.where(low, a, b) + jnp.where(low, pltpu.roll(a, SUBLANES - half, 2), pltpu.roll(b, half, 2))
        half //= 2
    return p[:, 0]


def _peer_kernel(idx0_ref, idxn_ref, u_ref, w_ref, x_ref, mod_ref, fg_ref, tab_hbm, out_ref, gbuf, xs, yacc, wsc, sem):
    tt = GATHER_TOKENS
    rows = tt * N_SEL
    d = x_ref.shape[1]
    s8 = SUBLANES
    groups = N_SEL // s8
    i = pl.program_id(0)
    n = pl.num_programs(0)

    def issue(idx_ref, slot, r_lo, r_hi):
        for r in range(r_lo, r_hi):
            e = idx_ref[0, 0, r]
            pltpu.make_async_copy(tab_hbm.at[pl.ds(e, 1)], gbuf.at[slot, pl.ds(r, 1)], sem.at[slot]).start(priority=r % 2)

    def wait_slot(slot):
        pltpu.make_async_copy(tab_hbm.at[pl.ds(0, rows)], gbuf.at[slot], sem.at[slot]).wait()

    @pl.when(i == 0)
    def _():
        issue(idx0_ref, 0, 0, rows)

    def step(cur, nxt):
        for t in range(tt):
            for s in range(s8):
                xs[t, s:s + 1, :] = u_ref[t:t + 1, LANES * s:LANES * (s + 1)]
        off = (i % (LANES // tt)) * tt
        wsc[...] = pltpu.roll(w_ref[...], (LANES - off) % LANES, 1)
        wait_slot(cur)
        half = s8 // 2
        for t in range(tt):
            xt = xs[t]
            dots = []
            for jb in range(groups):
                r0 = t * N_SEL + jb * s8
                issue(idxn_ref, nxt, r0, r0 + half)
                p = gbuf[cur, r0:r0 + s8, 0:s8, :].astype(F32) * xt
                dots.append(jnp.sum(_fold_sublanes(p[None]), axis=-1, keepdims=True)[0])
            wcol = wsc[:, t:t + 1]
            y = jnp.zeros((s8, LANES), F32)
            for jb in range(groups):
                r0 = t * N_SEL + jb * s8
                issue(idxn_ref, nxt, r0 + half, r0 + s8)
                act = _gelu(dots[jb]) * wcol[jb * s8:(jb + 1) * s8, :]
                v = gbuf[cur, r0:r0 + s8, s8:2 * s8, :].astype(F32)
                for jj in range(s8):
                    y = y + v[jj] * act[jj:jj + 1, :]
            for s in range(s8):
                yacc[t:t + 1, LANES * s:LANES * (s + 1)] = y[s:s + 1, :]

    @pl.when(i % 2 == 0)
    def _():
        step(0, 1)

    @pl.when(i % 2 == 1)
    def _():
        step(1, 0)

    @pl.when(i == n - 1)
    def _():
        wait_slot((i + 1) % 2)

    x2 = x_ref[...] + mod_ref[0] * yacc[...]
    out_ref[...] = _rms(x2, fg_ref[...])


def _peer(idx3, u, wts, x1, gate_mod, fg, uv_tab, seq_len, t0, n):
    d = x1.shape[1]
    tt = GATHER_TOKENS
    n_tiles = n // tt
    off = t0 // tt
    per_seq = seq_len // tt
    rows = tt * N_SEL
    smem_spec = lambda m: pl.BlockSpec((1, 1, rows), m, memory_space=pltpu.SMEM)
    return pl.pallas_call(
        _peer_kernel,
        grid=(n_tiles,),
        in_specs=[smem_spec(lambda i: (off, 0, 0)),
                  smem_spec(lambda i: (off + jnp.minimum(i + 1, n_tiles - 1), 0, 0)),
                  pl.BlockSpec((tt, d), lambda i: (off + i, 0)),
                  pl.BlockSpec((N_SEL, LANES), lambda i: (0, (off + i) // (LANES // tt))),
                  pl.BlockSpec((tt, d), lambda i: (off + i, 0)),
                  pl.BlockSpec((1, 1, d), lambda i: ((off + i) // per_seq, 0, 0)),
                  pl.BlockSpec((1, d), lambda i: (0, 0)),
                  pl.BlockSpec(memory_space=pl.ANY)],
        out_specs=pl.BlockSpec((tt, d), lambda i: (i, 0)),
        out_shape=jax.ShapeDtypeStruct((n, d), F32),
        scratch_shapes=[pltpu.VMEM((2, rows) + uv_tab.shape[1:], uv_tab.dtype), pltpu.VMEM((tt, SUBLANES, LANES), F32),
                        pltpu.VMEM((tt, d), F32), pltpu.VMEM((N_SEL, LANES), F32), pltpu.SemaphoreType.DMA((2,))],
        compiler_params=_cparams(("arbitrary",)),
        name="peer_gather",
    )(idx3, idx3, u, wts, x1, gate_mod, fg, uv_tab)


SC_LANES = 16
SC_CORES = 2
SC_SUBCORES = 16
SC_WORKERS = SC_CORES * SC_SUBCORES
SC_CHUNK = 32
SC_BLOCK = 4
SC_SEQUENCES = 5
TC_FIRST_SEQUENCES = 6


def _sc_params():
    import dataclasses
    cp = pltpu.CompilerParams()
    if "needs_layout_passes" in pltpu.CompilerParams.__dataclass_fields__:
        cp = dataclasses.replace(cp, needs_layout_passes=False)
    return cp


def _sc_mesh():
    from jax.experimental.pallas import tpu_sc as plsc
    return plsc.VectorSubcoreMesh(core_axis_name="c", subcore_axis_name="s", num_cores=SC_CORES, num_subcores=SC_SUBCORES)


def _sc_gather_pipeline(tab_hbm, idx_v, bufs, sems, n_chunks, compute):
    def copy(h, b):
        return pltpu.make_async_copy(tab_hbm.at[idx_v.at[pl.ds(h * SC_CHUNK, SC_CHUNK)]], bufs[b], sems[b])

    copy(0, 0).start()

    def pair(hp, carry):
        h0 = 2 * hp
        copy(h0 + 1, 1).start()
        copy(h0, 0).wait()
        compute(h0, bufs[0])

        @pl.when(hp + 1 < n_chunks // 2)
        def _():
            copy(h0 + 2, 0).start()
        copy(h0 + 1, 1).wait()
        compute(h0 + 1, bufs[1])
        return carry

    lax.fori_loop(0, n_chunks // 2, pair, 0)


def _sc_dots(u, idx_tm, tab, t0, n_tok):
    from jax.experimental.pallas import tpu_sc as plsc
    d = u.shape[1]
    n_sel = idx_tm.shape[1]
    per_w = n_tok // SC_WORKERS
    ln, rb_n, ch = SC_LANES, SC_BLOCK, SC_CHUNK
    tile = tab.shape[1:]
    cps = LANES // ln

    def body(u_hbm, idx_hbm, tab_hbm, out_hbm, x_v, idx_v, rows0, rows1, d_v, acc_v, sem0, sem1):
        wid = lax.axis_index("s") * SC_CORES + lax.axis_index("c")
        lane = lax.iota(jnp.int32, ln)

        def token(t, carry):
            loc = wid * per_w + t
            pltpu.sync_copy(u_hbm.at[t0 + loc], x_v)
            pltpu.sync_copy(idx_hbm.at[t0 + loc], idx_v)

            def compute(h, rows_v):
                def group(g, c2):
                    def quad(rb, c3):
                        accs = [jnp.zeros((ln,), F32) for _ in range(rb_n)]
                        for c in range(d // ln):
                            xv = x_v[pl.ds(c * ln, ln)]
                            for k in range(rb_n):
                                accs[k] = accs[k] + rows_v[g * ln + rb * rb_n + k, c // cps, pl.ds((c % cps) * ln, ln)] * xv
                        for k in range(rb_n):
                            acc_v[rb * rb_n + k, :] = accs[k]
                        return c3
                    lax.fori_loop(0, ln // rb_n, quad, 0)
                    dvec = jnp.zeros((ln,), F32)
                    for l in range(ln):
                        dvec = dvec + plsc.load_gather(acc_v, [lane, jnp.full((ln,), l, jnp.int32)])
                    d_v[pl.ds(h * ch + g * ln, ln)] = dvec
                    return c2
                lax.fori_loop(0, ch // ln, group, 0)

            _sc_gather_pipeline(tab_hbm, idx_v, (rows0, rows1), (sem0, sem1), n_sel // ch, compute)
            pltpu.sync_copy(d_v, out_hbm.at[loc])
            return carry

        lax.fori_loop(0, per_w, token, 0)

    return pl.kernel(
        body, out_type=jax.ShapeDtypeStruct((n_tok, n_sel), F32), mesh=_sc_mesh(),
        scratch_types=[pltpu.VMEM((d,), F32), pltpu.VMEM((n_sel,), jnp.int32),
                       pltpu.VMEM((ch,) + tile, F32), pltpu.VMEM((ch,) + tile, F32),
                       pltpu.VMEM((n_sel,), F32), pltpu.VMEM((ln, ln), F32),
                       pltpu.SemaphoreType.DMA, pltpu.SemaphoreType.DMA],
        compiler_params=_sc_params(), name="sc_dots")(u, idx_tm, tab)


def _sc_mix(act, idx_tm, tab, t0, n_tok):
    tile = tab.shape[1:]
    d = tile[0] * tile[1]
    n_sel = idx_tm.shape[1]
    per_w = n_tok // SC_WORKERS
    ln, cb_n, ch = SC_LANES, SC_BLOCK, SC_CHUNK
    bps = LANES // (cb_n * ln)

    def body(act_hbm, idx_hbm, tab_hbm, out_hbm, a_v, idx_v, rows0, rows1, sp_v, y_v, sem0, sem1):
        wid = lax.axis_index("s") * SC_CORES + lax.axis_index("c")
        lane = lax.iota(jnp.int32, ln)

        def token(t, carry):
            loc = wid * per_w + t
            pltpu.sync_copy(act_hbm.at[loc], a_v)
            pltpu.sync_copy(idx_hbm.at[t0 + loc], idx_v)

            def splat(g, c2):
                avec = a_v[pl.ds(g * ln, ln)]
                for r in range(ln):
                    sp_v[g * ln + r, :] = jnp.full((ln,), jnp.sum(jnp.where(lane == r, avec, 0.0)), F32)
                return c2
            lax.fori_loop(0, n_sel // ln, splat, 0)

            def zero(c, c2):
                y_v[pl.ds(c * ln, ln)] = jnp.zeros((ln,), F32)
                return c2
            lax.fori_loop(0, d // ln, zero, 0)

            def compute(h, rows_v):
                def colblock(cb, c2):
                    base = cb * (cb_n * ln)
                    accs = [y_v[pl.ds(base + k * ln, ln)] for k in range(cb_n)]
                    for r in range(ch):
                        ar = sp_v[h * ch + r, :]
                        for k in range(cb_n):
                            accs[k] = accs[k] + rows_v[r, cb // bps, pl.ds((cb % bps) * (cb_n * ln) + k * ln, ln)] * ar
                    for k in range(cb_n):
                        y_v[pl.ds(base + k * ln, ln)] = accs[k]
                    return c2
                lax.fori_loop(0, d // (cb_n * ln), colblock, 0)

            _sc_gather_pipeline(tab_hbm, idx_v, (rows0, rows1), (sem0, sem1), n_sel // ch, compute)
            pltpu.sync_copy(y_v, out_hbm.at[loc])
            return carry

        lax.fori_loop(0, per_w, token, 0)

    return pl.kernel(
        body, out_type=jax.ShapeDtypeStruct((n_tok, d), F32), mesh=_sc_mesh(),
        scratch_types=[pltpu.VMEM((n_sel,), F32), pltpu.VMEM((n_sel,), jnp.int32),
                       pltpu.VMEM((ch,) + tile, F32), pltpu.VMEM((ch,) + tile, F32),
                       pltpu.VMEM((n_sel, ln), F32), pltpu.VMEM((d,), F32),
                       pltpu.SemaphoreType.DMA, pltpu.SemaphoreType.DMA],
        compiler_params=_sc_params(), name="sc_mix")(act, idx_tm, tab)


def _act_kernel(d_ref, w_ref, after_ref, o_ref):
    del after_ref
    o_ref[...] = _gelu(d_ref[...]) * w_ref[...]


def _activation(dots, w_tm, after):
    n, k = dots.shape
    tm = min(ROW_TILE * 4, n)
    spec = pl.BlockSpec((tm, k), lambda i: (i, 0))
    anchor = pl.BlockSpec((SUBLANES, after.shape[1]), lambda i: (0, 0))
    return pl.pallas_call(_act_kernel, grid=(n // tm,), in_specs=[spec, spec, anchor], out_specs=spec,
                          out_shape=jax.ShapeDtypeStruct((n, k), F32),
                          compiler_params=_cparams(("arbitrary",)), name="peer_act")(dots, w_tm, after)


def _finish_kernel(x_ref, y_ref, mod_ref, fg_ref, o_ref):
    o_ref[...] = _rms(x_ref[...] + mod_ref[0] * y_ref[...], fg_ref[...])


def _finish(x1, y, gate_mod, fg, seq_len, t0):
    n, d = y.shape
    tm = min(ROW_TILE, seq_len)
    per_seq = seq_len // tm
    first = t0 // tm
    return pl.pallas_call(
        _finish_kernel,
        grid=(n // tm,),
        in_specs=[pl.BlockSpec((tm, d), lambda i: (first + i, 0)),
                  pl.BlockSpec((tm, d), lambda i: (i, 0)),
                  pl.BlockSpec((1, 1, d), lambda i: ((first + i) // per_seq, 0, 0)),
                  pl.BlockSpec((1, d), lambda i: (0, 0))],
        out_specs=pl.BlockSpec((tm, d), lambda i: (i, 0)),
        out_shape=jax.ShapeDtypeStruct((n, d), F32),
        compiler_params=_cparams(("arbitrary",)),
        name="peer_finish",
    )(x1, y, gate_mod, fg)


def _block_diag(w):
    g, i, j = w.shape
    eye = jnp.eye(g, dtype=w.dtype)
    return (eye[:, None, :, None] * w[:, :, None, :]).reshape(g * i, g * j)


def kernel(x, c, ctx, c_ctx, ada_w, ada_b, norm1_g, w_in, conv_w, conv_b, rg_w_a, rg_b_a, rg_w_x, rg_b_x, rg_lambda, gla_w_g, gla_b_g, gla_norm_g, w_out, norm2_g, peer_w_q, peer_keys, peer_u, peer_v, final_norm_g):
    bsz, t_lat, d = x.shape
    t_ctx = ctx.shape[1]
    assert ada_w.shape[0] == 1, "single-layer block"
    n = bsz * t_lat
    x2 = x.reshape(n, d)
    ctx2 = ctx.reshape(bsz * t_ctx, d)
    pos = _sincos_2d(t_lat, d)

    pad_rows = (-(bsz + 1)) % SUBLANES
    cc = jnp.concatenate([c, c_ctx[None, :], jnp.zeros((pad_rows, d), F32)], axis=0)
    mod = _adaln(cc, ada_w[0], ada_b[0][None, :])
    mod_l = mod[:bsz].reshape(bsz, N_MOD, 1, d)
    mod_c = mod[bsz].reshape(N_MOD, 1, 1, d)

    w_in_p = jnp.pad(w_in[0], ((0, 0), (0, IN_COLS_PAD - w_in.shape[2]))).astype(BF16)
    g1 = norm1_g[0][None, :]
    h_l = _inproj(x2, pos, mod_l[:, 0], mod_l[:, 1], g1, w_in_p, t_lat)
    h_c = _inproj(ctx2, None, mod_c[0], mod_c[1], g1, w_in_p, t_ctx)

    wg_rg = jnp.concatenate([_block_diag(rg_w_a[0, 0]), _block_diag(rg_w_x[0, 0]),
                             _block_diag(rg_w_a[0, 1]), _block_diag(rg_w_x[0, 1])], axis=1).astype(BF16)
    bg_rg = jnp.concatenate([rg_b_a[0, 0], rg_b_x[0, 0], rg_b_a[0, 1], rg_b_x[0, 1]])[None, :]
    y_rg = _rglru(h_l, h_c, conv_w[0], conv_b[0][None, :], wg_rg, bg_rg, rg_lambda[0], bsz, t_lat, t_ctx)

    wg_gla = jnp.zeros((2, LANES, QK_WIDTH), F32)
    wg_gla = wg_gla.at[0, 0:GLA_RANK].set(gla_w_g[0, 0]).at[1, GLA_RANK:2 * GLA_RANK].set(gla_w_g[0, 1])
    o_gla = _gla(h_l, h_c, wg_gla, gla_b_g[0][:, None, :], bsz, t_lat, t_ctx)

    x1 = _merge(y_rg, h_l, o_gla, x2, pos, mod_l[:, 2], gla_norm_g[0][None, :], w_out[0].astype(BF16), t_lat)

    half = PEER_DQ // 2
    wq = peer_w_q[0].reshape(d, PEER_HEADS, 2, half).transpose(0, 2, 1, 3).reshape(d, 2 * PEER_HEADS * half).astype(BF16)
    kt = peer_keys[0].transpose(1, 2, 0, 3)
    wk = (kt[:, :, :, None, :] * jnp.eye(PEER_HEADS, dtype=F32)[None, None, :, :, None])
    wk = wk.reshape(2, N_KEYS * PEER_HEADS, PEER_HEADS * half).astype(BF16)
    u, eidx, wts = _select(x1, mod_l[:, 3], mod_l[:, 4], norm2_g[0][None, :], wq, wk, t_lat)
    tt = GATHER_TOKENS
    idx_tm = eidx.T
    idx3 = idx_tm.reshape(n // tt, 1, tt * N_SEL)
    u_tiles = peer_u[0].reshape(peer_u.shape[1], d // LANES, LANES)
    v_tiles = peer_v[0].reshape(peer_v.shape[1], d // LANES, LANES)
    uv_tab = jnp.concatenate([u_tiles, v_tiles], axis=1).astype(BF16)
    fg = final_norm_g[None, :]
    sc_seqs = SC_SEQUENCES if (bsz > SC_SEQUENCES and t_lat % (4 * ROW_TILE) == 0) else 0
    n_tc = (bsz - sc_seqs) * t_lat
    gate = mod_l[:, 5]
    if sc_seqs:
        n_a = min(TC_FIRST_SEQUENCES, bsz - sc_seqs - 1) * t_lat
        n_sc = n - n_tc
        out_a = _peer(idx3, u, wts, x1, gate, fg, uv_tab, t_lat, 0, n_a)
        dots = _sc_dots(u, idx_tm, u_tiles, n_tc, n_sc)
        act = _activation(dots, wts[:, n_tc:].T, out_a)
        out_b = _peer(idx3, u, wts, x1, gate, fg, uv_tab, t_lat, n_a, n_tc - n_a)
        y_sc = _sc_mix(act, idx_tm, v_tiles, n_tc, n_sc)
        out = jnp.concatenate([out_a, out_b, _finish(x1, y_sc, gate, fg, t_lat, n_tc)], axis=0)
    else:
        out = _peer(idx3, u, wts, x1, gate, fg, uv_tab, t_lat, 0, n)
    return out.reshape(bsz, t_lat, d)
```

```python
import functools
import math

import jax
import jax.numpy as jnp
from jax import lax
from jax.experimental import pallas as pl
from jax.experimental.pallas import tpu as pltpu

F32 = jnp.float32
BF16 = jnp.bfloat16

GRID_W = 64
POS_THETA = 10000.0
RMS_EPS = 1e-6
N_MOD = 6
RG_WIDTH = 512
RG_BLOCKS = 8
RG_C = 8.0
CONV_W = 4
GLA_HEADS = 4
GLA_DK = 64
GLA_DV = 128
GLA_WIDTH = GLA_HEADS * GLA_DV
GLA_RANK = 16
GLA_GATE_NORM = 16.0
GLA_CHUNK = 64
QK_WIDTH = GLA_HEADS * GLA_DK
N_KEYS = 128
PEER_HEADS = 8
PEER_DQ = 256
PEER_TOPK = 16
N_SEL = PEER_HEADS * PEER_TOPK

LANES = 128
SUBLANES = 8
VMEM_LIMIT = 56 * 1024 * 1024

COL_RX, COL_RGATE, COL_Q, COL_K, COL_V, COL_R, COL_LR = 0, 512, 1024, 1280, 1536, 2048, 2560
IN_COLS_PAD = 2688

SCAN_BLOCK = 128
HALO = 8
ROW_TILE = 512
SEL_TILE = 256
GATHER_TOKENS = 8


def _cparams(sem, vmem=VMEM_LIMIT):
    return pltpu.CompilerParams(dimension_semantics=sem, vmem_limit_bytes=vmem)


def _sincos_2d(n_tokens, dim):
    rows = n_tokens // GRID_W
    r, col = jnp.meshgrid(jnp.arange(rows, dtype=F32), jnp.arange(GRID_W, dtype=F32), indexing="ij")
    quarter = dim // 4
    omega = POS_THETA ** (-jnp.arange(quarter, dtype=F32) / quarter)

    def axis_embed(p):
        ang = p.reshape(-1)[:, None] * omega[None, :]
        return jnp.concatenate([jnp.sin(ang), jnp.cos(ang)], axis=-1)

    return jnp.concatenate([axis_embed(r), axis_embed(col)], axis=-1)


def _gelu(x):
    return 0.5 * x * (1.0 + lax.erf(x * (1.0 / math.sqrt(2.0))))


def _rms(x, g):
    ms = jnp.mean(x * x, axis=-1, keepdims=True)
    return x * lax.rsqrt(ms + RMS_EPS) * g


def _adaln_kernel(c_ref, w_ref, b_ref, o_ref):
    a = jax.nn.silu(c_ref[...]).astype(BF16)
    o_ref[...] = jnp.dot(a, w_ref[...].astype(BF16), preferred_element_type=F32) + b_ref[...]


def _adaln(cc, w, b):
    rows, d = cc.shape
    n = w.shape[1]
    tn = 1536
    return pl.pallas_call(
        _adaln_kernel,
        grid=(n // tn,),
        in_specs=[pl.BlockSpec((rows, d), lambda j: (0, 0)),
                  pl.BlockSpec((d, tn), lambda j: (0, j)),
                  pl.BlockSpec((1, tn), lambda j: (0, j))],
        out_specs=pl.BlockSpec((rows, tn), lambda j: (0, j)),
        out_shape=jax.ShapeDtypeStruct((rows, n), F32),
        compiler_params=_cparams(("arbitrary",)),
        name="adaln",
    )(cc, w, b)


def _inproj_kernel(*refs, has_pos):
    if has_pos:
        x_ref, pos_ref, shift_ref, scale_ref, g_ref, w_ref, o_ref = refs
        x = x_ref[...] + pos_ref[...]
    else:
        x_ref, shift_ref, scale_ref, g_ref, w_ref, o_ref = refs
        x = x_ref[...]
    y = _rms(x, g_ref[...]) * (1.0 + scale_ref[0]) + shift_ref[0]
    o_ref[...] = jnp.dot(y.astype(BF16), w_ref[...], preferred_element_type=F32)


def _inproj(x2, pos, shift, scale, g, w, seq_len):
    n, d = x2.shape
    cols = w.shape[1]
    tm = min(ROW_TILE, seq_len)
    per_seq = seq_len // tm
    shared = shift.shape[0] == 1
    mod_map = (lambda i: (0, 0, 0)) if shared else (lambda i: (i // per_seq, 0, 0))
    in_specs = [pl.BlockSpec((tm, d), lambda i: (i, 0))]
    args = [x2]
    if pos is not None:
        in_specs.append(pl.BlockSpec((tm, d), lambda i: (i % per_seq, 0)))
        args.append(pos)
    in_specs += [pl.BlockSpec((1, 1, d), mod_map), pl.BlockSpec((1, 1, d), mod_map),
                 pl.BlockSpec((1, d), lambda i: (0, 0)), pl.BlockSpec((d, cols), lambda i: (0, 0))]
    args += [shift, scale, g, w]
    return pl.pallas_call(
        functools.partial(_inproj_kernel, has_pos=pos is not None),
        grid=(n // tm,),
        in_specs=in_specs,
        out_specs=pl.BlockSpec((tm, cols), lambda i: (i, 0)),
        out_shape=jax.ShapeDtypeStruct((n, cols), F32),
        compiler_params=_cparams(("arbitrary",)),
        name="inproj",
    )(*args)


def _rglru_kernel(rxl_ref, rxc_ref, cw_ref, cb_ref, wg_ref, bg_ref, lam_ref, y_ref, xpl_ref, xpc_ref, *, t_lat, t_ctx):
    tb = SCAN_BLOCK
    w = RG_WIDTH
    zeros_halo = jnp.zeros((HALO, w), F32)
    for xp_ref, src_ref, t in ((xpl_ref, rxl_ref, t_lat), (xpc_ref, rxc_ref, t_ctx)):
        xp_ref[0:HALO, :] = zeros_halo
        xp_ref[HALO:HALO + t, :] = src_ref[...]
        xp_ref[HALO + t:HALO + t + HALO, :] = zeros_halo

    cw = cw_ref[...]
    cb = cb_ref[...]
    row = lax.broadcasted_iota(jnp.int32, (tb, w), 0)

    def block(xp_ref, r0, d, h_in):
        ext = xp_ref[pl.ds(r0, tb + 2 * HALO), :]
        xc = (ext[HALO - 2:HALO - 2 + tb] * cw[0:1] + ext[HALO - 1:HALO - 1 + tb] * cw[1:2]
              + ext[HALO:HALO + tb] * cw[2:3] + ext[HALO + 1:HALO + 1 + tb] * cw[3:4] + cb)
        g = jnp.dot(xc.astype(BF16), wg_ref[:, 2 * w * d:2 * w * (d + 1)], preferred_element_type=F32)
        g = g + bg_ref[:, 2 * w * d:2 * w * (d + 1)]
        gate_r = jax.nn.sigmoid(g[:, :w])
        gate_i = jax.nn.sigmoid(g[:, w:])
        log_a = (-RG_C * jax.nn.softplus(-lam_ref[d:d + 1, :])) * gate_r
        a = jnp.exp(log_a)
        th = jnp.tanh(log_a)
        bv = jnp.sqrt(-2.0 * th / (1.0 - th)) * gate_i * xc
        s = 1
        while s < tb:
            if d == 0:
                valid = row >= s
                a_p = jnp.where(valid, pltpu.roll(a, s, 0), 1.0)
                b_p = jnp.where(valid, pltpu.roll(bv, s, 0), 0.0)
            else:
                valid = row < tb - s
                a_p = jnp.where(valid, pltpu.roll(a, tb - s, 0), 1.0)
                b_p = jnp.where(valid, pltpu.roll(bv, tb - s, 0), 0.0)
            bv = a * b_p + bv
            a = a * a_p
            s *= 2
        h = a * h_in + bv
        h_out = h[tb - 1:tb] if d == 0 else h[0:1]
        return h, h_out

    n_lat = t_lat // tb
    n_ctx = t_ctx // tb
    h0 = jnp.zeros((1, w), F32)

    y_ref[...] = jnp.zeros(y_ref.shape, F32)

    def both_ctx(i, hs):
        return (block(xpc_ref, pl.multiple_of(i * tb, tb), 0, hs[0])[1],
                block(xpc_ref, pl.multiple_of((n_ctx - 1 - i) * tb, tb), 1, hs[1])[1])

    def both_lat(i, hs):
        rf = pl.multiple_of(i * tb, tb)
        hf, hf_out = block(xpl_ref, rf, 0, hs[0])
        y_ref[pl.ds(rf, tb), :] = y_ref[pl.ds(rf, tb), :] + hf
        rb = pl.multiple_of((n_lat - 1 - i) * tb, tb)
        hb, hb_out = block(xpl_ref, rb, 1, hs[1])
        y_ref[pl.ds(rb, tb), :] = y_ref[pl.ds(rb, tb), :] + hb
        return hf_out, hb_out

    lax.fori_loop(0, n_lat, both_lat, lax.fori_loop(0, n_ctx, both_ctx, (h0, h0)))


def _rglru(h_l, h_c, conv_w, conv_b, wg, bg, lam, batch, t_lat, t_ctx):
    w = RG_WIDTH
    kern = functools.partial(_rglru_kernel, t_lat=t_lat, t_ctx=t_ctx)
    full = lambda shape: pl.BlockSpec(shape, lambda b: tuple(0 for _ in shape))
    return pl.pallas_call(
        kern,
        grid=(batch,),
        in_specs=[pl.BlockSpec((t_lat, w), lambda b: (b, COL_RX // w)),
                  pl.BlockSpec((t_ctx, w), lambda b: (b, COL_RX // w)),
                  full((CONV_W, w)), full((1, w)), full((w, 4 * w)), full((1, 4 * w)), full((2, w))],
        out_specs=pl.BlockSpec((t_lat, w), lambda b: (b, 0)),
        out_shape=jax.ShapeDtypeStruct((batch * t_lat, w), F32),
        scratch_shapes=[pltpu.VMEM((t_lat + 2 * HALO, w), F32), pltpu.VMEM((t_ctx + 2 * HALO, w), F32)],
        compiler_params=_cparams(("arbitrary",)),
        name="rglru",
    )(h_l, h_c, conv_w, conv_b, wg, bg, lam)


def _gla_kernel(ql_ref, kl_ref, vl_ref, lrl_ref, kc_ref, vc_ref, lrc_ref, wg_ref, bg_ref, o_ref, *, t_lat, t_ctx):
    c = GLA_CHUNK
    qk = QK_WIDTH
    scale = GLA_DK ** -0.5
    row = lax.broadcasted_iota(jnp.int32, (c, qk), 0)
    ri = lax.broadcasted_iota(jnp.int32, (c, c), 0)
    ci = lax.broadcasted_iota(jnp.int32, (c, c), 1)

    def decays(lr_ref, r0, d):
        z = jnp.dot(lr_ref[pl.ds(r0, c), :].astype(BF16), wg_ref[d].astype(BF16), preferred_element_type=F32) + bg_ref[d]
        b = jax.nn.log_sigmoid(z) * (1.0 / GLA_GATE_NORM)
        s = 1
        while s < c:
            if d == 0:
                b = b + jnp.where(row >= s, pltpu.roll(b, s, 0), 0.0)
            else:
                b = b + jnp.where(row < c - s, pltpu.roll(b, c - s, 0), 0.0)
            s *= 2
        b_last = b[c - 1:c] if d == 0 else b[0:1]
        return b, b_last

    def state_update(st, k, v, b, b_last):
        k_dec = (k * jnp.exp(b_last - b)).astype(BF16)
        vb = v.astype(BF16)
        parts = []
        for h in range(GLA_HEADS):
            parts.append(lax.dot_general(vb[:, GLA_DV * h:GLA_DV * (h + 1)], k_dec[:, GLA_DK * h:GLA_DK * (h + 1)],
                                         (((0,), (0,)), ((), ())), preferred_element_type=F32))
        return st * jnp.exp(b_last) + jnp.concatenate(parts, axis=1)

    def ctx_chunk(r0, d, st):
        b, b_last = decays(lrc_ref, r0, d)
        return state_update(st, kc_ref[pl.ds(r0, c), :], vc_ref[pl.ds(r0, c), :], b, b_last)

    def lat_chunk(r0, d, st):
        b, b_last = decays(lrl_ref, r0, d)
        q = ql_ref[pl.ds(r0, c), :] * scale
        k = kl_ref[pl.ds(r0, c), :]
        v = vl_ref[pl.ds(r0, c), :]
        q_dec = (q * jnp.exp(b)).astype(BF16)
        k_inv = (k * jnp.exp(-b)).astype(BF16)
        vb = v.astype(BF16)
        stb = st.astype(BF16)
        keep = (ci <= ri) if d == 0 else (ci >= ri)
        outs = []
        for h in range(GLA_HEADS):
            ks = slice(GLA_DK * h, GLA_DK * (h + 1))
            vs = slice(GLA_DV * h, GLA_DV * (h + 1))
            sc = lax.dot_general(q_dec[:, ks], k_inv[:, ks], (((1,), (1,)), ((), ())), preferred_element_type=F32)
            sc = jnp.where(keep, sc, 0.0).astype(BF16)
            o_h = jnp.dot(sc, vb[:, vs], preferred_element_type=F32)
            o_h = o_h + lax.dot_general(q_dec[:, ks], stb[:, ks], (((1,), (1,)), ((), ())), preferred_element_type=F32)
            outs.append(o_h)
        o = jnp.concatenate(outs, axis=1)
        o_ref[pl.ds(r0, c), :] = o_ref[pl.ds(r0, c), :] + o
        return state_update(st, k, v, b, b_last)

    n_lat = t_lat // c
    n_ctx = t_ctx // c
    st0 = jnp.zeros((GLA_DV, qk), F32)
    o_ref[...] = jnp.zeros(o_ref.shape, F32)

    def both_ctx(i, carry):
        return (ctx_chunk(pl.multiple_of(i * c, c), 0, carry[0]),
                ctx_chunk(pl.multiple_of((n_ctx - 1 - i) * c, c), 1, carry[1]))

    def both_lat(i, carry):
        return (lat_chunk(pl.multiple_of(i * c, c), 0, carry[0]),
                lat_chunk(pl.multiple_of((n_lat - 1 - i) * c, c), 1, carry[1]))

    lax.fori_loop(0, n_lat, both_lat, lax.fori_loop(0, n_ctx, both_ctx, (st0, st0)))


def _gla(h_l, h_c, wg_pad, bg, batch, t_lat, t_ctx):
    kern = functools.partial(_gla_kernel, t_lat=t_lat, t_ctx=t_ctx)
    qk = QK_WIDTH
    col = lambda t, width, off: pl.BlockSpec((t, width), lambda b: (b, off // width))
    return pl.pallas_call(
        kern,
        grid=(batch,),
        in_specs=[col(t_lat, qk, COL_Q), col(t_lat, qk, COL_K), col(t_lat, GLA_WIDTH, COL_V), col(t_lat, LANES, COL_LR),
                  col(t_ctx, qk, COL_K), col(t_ctx, GLA_WIDTH, COL_V), col(t_ctx, LANES, COL_LR),
                  pl.BlockSpec((2, LANES, qk), lambda b: (0, 0, 0)), pl.BlockSpec((2, 1, qk), lambda b: (0, 0, 0))],
        out_specs=pl.BlockSpec((t_lat, GLA_WIDTH), lambda b: (b, 0)),
        out_shape=jax.ShapeDtypeStruct((batch * t_lat, GLA_WIDTH), F32),
        compiler_params=_cparams(("arbitrary",)),
        name="gla",
    )(h_l, h_l, h_l, h_l, h_c, h_c, h_c, wg_pad, bg)


def _merge_kernel(y_ref, gate_ref, o_ref, r_ref, x_ref, pos_ref, mod_ref, ng_ref, w_ref, out_ref):
    rg = y_ref[...] * _gelu(gate_ref[...])
    og = o_ref[...]
    heads = [_rms(og[:, GLA_DV * h:GLA_DV * (h + 1)], ng_ref[...]) for h in range(GLA_HEADS)]
    gla = jnp.concatenate(heads, axis=1) * jax.nn.silu(r_ref[...])
    m = jnp.dot(rg.astype(BF16), w_ref[0:RG_WIDTH, :], preferred_element_type=F32)
    m = m + jnp.dot(gla.astype(BF16), w_ref[RG_WIDTH:, :], preferred_element_type=F32)
    out_ref[...] = x_ref[...] + pos_ref[...] + mod_ref[0] * m


def _merge(y_rg, h_l, o_gla, x2, pos, gate_mod, ng, w_out, seq_len):
    n, d = x2.shape
    tm = min(ROW_TILE, seq_len)
    per_seq = seq_len // tm
    w = RG_WIDTH
    return pl.pallas_call(
        _merge_kernel,
        grid=(n // tm,),
        in_specs=[pl.BlockSpec((tm, w), lambda i: (i, 0)),
                  pl.BlockSpec((tm, w), lambda i: (i, COL_RGATE // w)),
                  pl.BlockSpec((tm, GLA_WIDTH), lambda i: (i, 0)),
                  pl.BlockSpec((tm, GLA_WIDTH), lambda i: (i, COL_R // GLA_WIDTH)),
                  pl.BlockSpec((tm, d), lambda i: (i, 0)),
                  pl.BlockSpec((tm, d), lambda i: (i % per_seq, 0)),
                  pl.BlockSpec((1, 1, d), lambda i: (i // per_seq, 0, 0)),
                  pl.BlockSpec((1, GLA_DV), lambda i: (0, 0)),
                  pl.BlockSpec((w + GLA_WIDTH, d), lambda i: (0, 0))],
        out_specs=pl.BlockSpec((tm, d), lambda i: (i, 0)),
        out_shape=jax.ShapeDtypeStruct((n, d), F32),
        compiler_params=_cparams(("arbitrary",)),
        name="merge",
    )(y_rg, h_l, o_gla, h_l, x2, pos, gate_mod, ng, w_out)


def _sort_network(n):
    pairs = []
    p = 1
    while p < n:
        k = p
        while k >= 1:
            for j in range(k % p, n - k, 2 * k):
                for i in range(min(k, n - j - k)):
                    if (i + j) // (2 * p) == (i + j + k) // (2 * p):
                        pairs.append((i + j, i + j + k))
            k //= 2
        p *= 2
    return pairs


def _bitonic_network(n):
    pairs = []
    s = n // 2
    while s >= 1:
        pairs += [(i, i + s) for i in range(n) if (i & s) == 0]
        s //= 2
    return pairs


SORT16 = _sort_network(PEER_TOPK)
BITONIC16 = _bitonic_network(PEER_TOPK)
TAG_BITS = 14
PAD_TAG = 1 << 30


def _beats(a, b):
    return (a[0] > b[0]) | ((a[0] == b[0]) & (a[1] < b[1]))


def _exchange(w, i, j):
    a, b = w[i], w[j]
    c = _beats(a, b)
    w[i] = (jnp.maximum(a[0], b[0]), jnp.where(c, a[1], b[1]))
    w[j] = (jnp.minimum(a[0], b[0]), jnp.where(c, b[1], a[1]))


def _sorted16(w):
    w = list(w)
    for i, j in SORT16:
        _exchange(w, i, j)
    return w


def _merge_top16(a, b):
    w = []
    for i in range(PEER_TOPK):
        x, y = a[i], b[PEER_TOPK - 1 - i]
        c = _beats(x, y)
        w.append((jnp.maximum(x[0], y[0]), jnp.where(c, x[1], y[1])))
    for i, j in BITONIC16:
        _exchange(w, i, j)
    return w


def _top16_of(elems):
    lists = [_sorted16(elems[i:i + PEER_TOPK]) for i in range(0, len(elems), PEER_TOPK)]
    while len(lists) > 1:
        nxt = [_merge_top16(lists[i], lists[i + 1]) for i in range(0, len(lists) - 1, 2)]
        if len(lists) % 2:
            nxt.append(lists[-1])
        lists = nxt
    return lists[0]


def _product_key_top16(scores1, scores2):
    shape = scores1[0].shape
    tops = []
    for scores in (scores1, scores2):
        tops.append(_top16_of([(s, jnp.full(shape, k, jnp.int32)) for k, s in enumerate(scores)]))
    s1, s2 = tops
    cand = []
    for a in range(PEER_TOPK):
        for b in range(PEER_TOPK // (a + 1)):
            tag = (s1[a][1] * N_KEYS + s2[b][1]) + ((a * PEER_TOPK + b) << TAG_BITS)
            cand.append((s1[a][0] + s2[b][0], tag))
    pad = (jnp.full(shape, -jnp.inf, F32), jnp.full(shape, PAD_TAG, jnp.int32))
    cand += [pad] * ((-len(cand)) % PEER_TOPK)
    top = _top16_of(cand)
    ex = [jnp.exp(t[0] - top[0][0]) for t in top]
    tot = ex[0]
    for e in ex[1:]:
        tot = tot + e
    inv = 1.0 / tot
    return [t[1] & ((1 << TAG_BITS) - 1) for t in top], [e * inv for e in ex]


def _select_kernel(x_ref, shift_ref, scale_ref, g_ref, wq_ref, wk_ref, u_ref, e_ref, w_ref, s_scr):
    u = _rms(x_ref[...], g_ref[...]) * (1.0 + scale_ref[0]) + shift_ref[0]
    u_ref[...] = u
    q = jnp.dot(u.astype(BF16), wq_ref[...], preferred_element_type=F32).astype(BF16)
    hd = q.shape[1] // 2
    for p in range(2):
        s_scr[p] = lax.dot_general(wk_ref[p], q[:, p * hd:(p + 1) * hd], (((1,), (1,)), ((), ())),
                                   preferred_element_type=F32)
    h8 = PEER_HEADS

    def lane_group(gi, carry):
        lanes = pl.ds(pl.multiple_of(gi * LANES, LANES), LANES)
        slabs = [[s_scr[p, h8 * k:h8 * (k + 1), lanes] for k in range(N_KEYS)] for p in range(2)]
        experts, weights = _product_key_top16(slabs[0], slabs[1])
        for k in range(PEER_TOPK):
            e_ref[h8 * k:h8 * (k + 1), lanes] = experts[k]
            w_ref[h8 * k:h8 * (k + 1), lanes] = weights[k]
        return carry

    lax.fori_loop(0, x_ref.shape[0] // LANES, lane_group, 0)


def _select(x1, shift, scale, g, wq, wk, seq_len):
    n, d = x1.shape
    tq = min(SEL_TILE, seq_len)
    per_seq = seq_len // tq
    mod_map = lambda i: (i // per_seq, 0, 0)
    return pl.pallas_call(
        _select_kernel,
        grid=(n // tq,),
        in_specs=[pl.BlockSpec((tq, d), lambda i: (i, 0)),
                  pl.BlockSpec((1, 1, d), mod_map), pl.BlockSpec((1, 1, d), mod_map),
                  pl.BlockSpec((1, d), lambda i: (0, 0)),
                  pl.BlockSpec(wq.shape, lambda i: (0, 0)),
                  pl.BlockSpec(wk.shape, lambda i: (0, 0, 0))],
        out_specs=[pl.BlockSpec((tq, d), lambda i: (i, 0)),
                   pl.BlockSpec((N_SEL, tq), lambda i: (0, i)),
                   pl.BlockSpec((N_SEL, tq), lambda i: (0, i))],
        out_shape=[jax.ShapeDtypeStruct((n, d), F32),
                   jax.ShapeDtypeStruct((N_SEL, n), jnp.int32),
                   jax.ShapeDtypeStruct((N_SEL, n), F32)],
        scratch_shapes=[pltpu.VMEM((2, N_KEYS * PEER_HEADS, tq), F32)],
        compiler_params=_cparams(("arbitrary",)),
        name="peer_select",
    )(x1, shift, scale, g, wq, wk)


def _fold_sublanes(p):
    sub = lax.broadcasted_iota(jnp.int32, p.shape[2:], 0)
    half = SUBLANES // 2
    while half >= 1:
        k = p.shape[1] // 2
        a, b = p[:, :k], p[:, k:]
        low = (sub & half) == 0
        p = jnp.where(low, a, b) + jnp.where(low, pltpu.roll(a, SUBLANES - half, 2), pltpu.roll(b, half, 2))
        half //= 2
    return p[:, 0]


def _peer_kernel(idx0_ref, idxn_ref, u_ref, w_ref, x_ref, mod_ref, fg_ref, tab_hbm, out_ref, gbuf, xs, yacc, wsc, sem):
    tt = GATHER_TOKENS
    rows = tt * N_SEL
    d = x_ref.shape[1]
    s8 = SUBLANES
    groups = N_SEL // s8
    i = pl.program_id(0)
    n = pl.num_programs(0)

    def issue(idx_ref, slot, r_lo, r_hi):
        for r in range(r_lo, r_hi):
            e = idx_ref[0, 0, r]
            pltpu.make_async_copy(tab_hbm.at[pl.ds(e, 1)], gbuf.at[slot, pl.ds(r, 1)], sem.at[slot]).start(priority=r % 2)

    def wait_slot(slot):
        pltpu.make_async_copy(tab_hbm.at[pl.ds(0, rows)], gbuf.at[slot], sem.at[slot]).wait()

    @pl.when(i == 0)
    def _():
        issue(idx0_ref, 0, 0, rows)

    def step(cur, nxt):
        for t in range(tt):
            for s in range(s8):
                xs[t, s:s + 1, :] = u_ref[t:t + 1, LANES * s:LANES * (s + 1)]
        off = (i % (LANES // tt)) * tt
        wsc[...] = pltpu.roll(w_ref[...], (LANES - off) % LANES, 1)
        wait_slot(cur)
        half = s8 // 2
        for t in range(tt):
            xt = xs[t]
            dots = []
            for jb in range(groups):
                r0 = t * N_SEL + jb * s8
                issue(idxn_ref, nxt, r0, r0 + half)
                p = gbuf[cur, r0:r0 + s8, 0:s8, :].astype(F32) * xt
                dots.append(jnp.sum(_fold_sublanes(p[None]), axis=-1, keepdims=True)[0])
            wcol = wsc[:, t:t + 1]
            y = jnp.zeros((s8, LANES), F32)
            for jb in range(groups):
                r0 = t * N_SEL + jb * s8
                issue(idxn_ref, nxt, r0 + half, r0 + s8)
                act = _gelu(dots[jb]) * wcol[jb * s8:(jb + 1) * s8, :]
                v = gbuf[cur, r0:r0 + s8, s8:2 * s8, :].astype(F32)
                for jj in range(s8):
                    y = y + v[jj] * act[jj:jj + 1, :]
            for s in range(s8):
                yacc[t:t + 1, LANES * s:LANES * (s + 1)] = y[s:s + 1, :]

    @pl.when(i % 2 == 0)
    def _():
        step(0, 1)

    @pl.when(i % 2 == 1)
    def _():
        step(1, 0)

    @pl.when(i == n - 1)
    def _():
        wait_slot((i + 1) % 2)

    x2 = x_ref[...] + mod_ref[0] * yacc[...]
    out_ref[...] = _rms(x2, fg_ref[...])


def _peer(idx3, u, wts, x1, gate_mod, fg, uv_tab, seq_len, t0, n):
    d = x1.shape[1]
    tt = GATHER_TOKENS
    n_tiles = n // tt
    off = t0 // tt
    per_seq = seq_len // tt
    rows = tt * N_SEL
    smem_spec = lambda m: pl.BlockSpec((1, 1, rows), m, memory_space=pltpu.SMEM)
    return pl.pallas_call(
        _peer_kernel,
        grid=(n_tiles,),
        in_specs=[smem_spec(lambda i: (off, 0, 0)),
                  smem_spec(lambda i: (off + jnp.minimum(i + 1, n_tiles - 1), 0, 0)),
                  pl.BlockSpec((tt, d), lambda i: (off + i, 0)),
                  pl.BlockSpec((N_SEL, LANES), lambda i: (0, (off + i) // (LANES // tt))),
                  pl.BlockSpec((tt, d), lambda i: (off + i, 0)),
                  pl.BlockSpec((1, 1, d), lambda i: ((off + i) // per_seq, 0, 0)),
                  pl.BlockSpec((1, d), lambda i: (0, 0)),
                  pl.BlockSpec(memory_space=pl.ANY)],
        out_specs=pl.BlockSpec((tt, d), lambda i: (i, 0)),
        out_shape=jax.ShapeDtypeStruct((n, d), F32),
        scratch_shapes=[pltpu.VMEM((2, rows) + uv_tab.shape[1:], uv_tab.dtype), pltpu.VMEM((tt, SUBLANES, LANES), F32),
                        pltpu.VMEM((tt, d), F32), pltpu.VMEM((N_SEL, LANES), F32), pltpu.SemaphoreType.DMA((2,))],
        compiler_params=_cparams(("arbitrary",)),
        name="peer_gather",
    )(idx3, idx3, u, wts, x1, gate_mod, fg, uv_tab)


SC_LANES = 16
SC_CORES = 2
SC_SUBCORES = 16
SC_WORKERS = SC_CORES * SC_SUBCORES
SC_CHUNK = 32
SC_BLOCK = 4
SC_SEQUENCES = 7
TC_FIRST_SEQUENCES = 5


def _sc_params():
    import dataclasses
    cp = pltpu.CompilerParams()
    if "needs_layout_passes" in pltpu.CompilerParams.__dataclass_fields__:
        cp = dataclasses.replace(cp, needs_layout_passes=False)
    return cp


def _bf16_low(w):
    return lax.bitcast_convert_type(w << 16, F32)


def _bf16_high(w):
    return lax.bitcast_convert_type(w & (-65536), F32)


def _pack_bf16_rows(tab):
    rows, d = tab.shape
    return lax.bitcast_convert_type(tab.astype(BF16).reshape(rows, d // 2, 2), jnp.int32)


def _sc_mesh():
    from jax.experimental.pallas import tpu_sc as plsc
    return plsc.VectorSubcoreMesh(core_axis_name="c", subcore_axis_name="s", num_cores=SC_CORES, num_subcores=SC_SUBCORES)


def _sc_gather_pipeline(tab_hbm, idx_v, bufs, sems, n_chunks, compute):
    def copy(h, b):
        return pltpu.make_async_copy(tab_hbm.at[idx_v.at[pl.ds(h * SC_CHUNK, SC_CHUNK)]], bufs[b], sems[b])

    copy(0, 0).start()

    def pair(hp, carry):
        h0 = 2 * hp
        copy(h0 + 1, 1).start()
        copy(h0, 0).wait()
        compute(h0, bufs[0])

        @pl.when(hp + 1 < n_chunks // 2)
        def _():
            copy(h0 + 2, 0).start()
        copy(h0 + 1, 1).wait()
        compute(h0 + 1, bufs[1])
        return carry

    lax.fori_loop(0, n_chunks // 2, pair, 0)


def _sc_dots(u, idx_tm, tab, t0, n_tok):
    from jax.experimental.pallas import tpu_sc as plsc
    d = u.shape[1]
    n_sel = idx_tm.shape[1]
    per_w = n_tok // SC_WORKERS
    ln, rb_n, ch = SC_LANES, SC_BLOCK, SC_CHUNK
    words = tab.shape[1]

    def body(u_hbm, idx_hbm, tab_hbm, out_hbm, x_v, idx_v, rows0, rows1, d_v, acc_v, sem0, sem1):
        wid = lax.axis_index("s") * SC_CORES + lax.axis_index("c")
        lane = lax.iota(jnp.int32, ln)

        def token(t, carry):
            loc = wid * per_w + t
            pltpu.sync_copy(u_hbm.at[t0 + loc], x_v)
            pltpu.sync_copy(idx_hbm.at[t0 + loc], idx_v)

            def compute(h, rows_v):
                def colblock(cb, accs):
                    base = cb * (rb_n * ln)
                    wbase = cb * (rb_n // 2 * ln)
                    xe = [plsc.load_gather(x_v, [base + k * 2 * ln + 2 * lane]) for k in range(rb_n // 2)]
                    xo = [plsc.load_gather(x_v, [base + k * 2 * ln + 2 * lane + 1]) for k in range(rb_n // 2)]
                    out = []
                    for r in range(ch):
                        acc = accs[r]
                        for k in range(rb_n // 2):
                            wv = rows_v[r, pl.ds(wbase + k * ln, ln)]
                            acc = acc + _bf16_low(wv) * xe[k] + _bf16_high(wv) * xo[k]
                        out.append(acc)
                    return tuple(out)
                accs = lax.fori_loop(0, d // (rb_n * ln), colblock, tuple(jnp.zeros((ln,), F32) for _ in range(ch)))
                for g in range(ch // ln):
                    for r in range(ln):
                        acc_v[r, :] = accs[g * ln + r]
                    dvec = jnp.zeros((ln,), F32)
                    for l in range(ln):
                        dvec = dvec + plsc.load_gather(acc_v, [lane, jnp.full((ln,), l, jnp.int32)])
                    d_v[pl.ds(h * ch + g * ln, ln)] = dvec

            _sc_gather_pipeline(tab_hbm, idx_v, (rows0, rows1), (sem0, sem1), n_sel // ch, compute)
            pltpu.sync_copy(d_v, out_hbm.at[loc])
            return carry

        lax.fori_loop(0, per_w, token, 0)

    return pl.kernel(
        body, out_type=jax.ShapeDtypeStruct((n_tok, n_sel), F32), mesh=_sc_mesh(),
        scratch_types=[pltpu.VMEM((d,), F32), pltpu.VMEM((n_sel,), jnp.int32),
                       pltpu.VMEM((ch, words), jnp.int32), pltpu.VMEM((ch, words), jnp.int32),
                       pltpu.VMEM((n_sel,), F32), pltpu.VMEM((ln, ln), F32),
                       pltpu.SemaphoreType.DMA, pltpu.SemaphoreType.DMA],
        compiler_params=_sc_params(), name="sc_dots")(u, idx_tm, tab)


def _sc_mix(act, idx_tm, tab, t0, n_tok):
    from jax.experimental.pallas import tpu_sc as plsc
    words = tab.shape[1]
    d = 2 * words
    n_sel = idx_tm.shape[1]
    per_w = n_tok // SC_WORKERS
    ln, cb_n, ch = SC_LANES, SC_BLOCK, SC_CHUNK

    def body(act_hbm, idx_hbm, tab_hbm, out_hbm, a_v, idx_v, rows0, rows1, sp_v, y_v, ye_v, yo_v, sem0, sem1):
        wid = lax.axis_index("s") * SC_CORES + lax.axis_index("c")
        lane = lax.iota(jnp.int32, ln)

        def token(t, carry):
            loc = wid * per_w + t
            pltpu.sync_copy(act_hbm.at[loc], a_v)
            pltpu.sync_copy(idx_hbm.at[t0 + loc], idx_v)

            def splat(g, c2):
                avec = a_v[pl.ds(g * ln, ln)]
                for r in range(ln):
                    sp_v[g * ln + r, :] = jnp.full((ln,), jnp.sum(jnp.where(lane == r, avec, 0.0)), F32)
                return c2
            lax.fori_loop(0, n_sel // ln, splat, 0)

            def zero(c, c2):
                ye_v[pl.ds(c * ln, ln)] = jnp.zeros((ln,), F32)
                yo_v[pl.ds(c * ln, ln)] = jnp.zeros((ln,), F32)
                return c2
            lax.fori_loop(0, words // ln, zero, 0)

            def compute(h, rows_v):
                def colblock(cb, c2):
                    wbase = cb * (cb_n // 2 * ln)
                    ev = [ye_v[pl.ds(wbase + k * ln, ln)] for k in range(cb_n // 2)]
                    od = [yo_v[pl.ds(wbase + k * ln, ln)] for k in range(cb_n // 2)]
                    for r in range(ch):
                        ar = sp_v[h * ch + r, :]
                        for k in range(cb_n // 2):
                            wv = rows_v[r, pl.ds(wbase + k * ln, ln)]
                            ev[k] = ev[k] + _bf16_low(wv) * ar
                            od[k] = od[k] + _bf16_high(wv) * ar
                    for k in range(cb_n // 2):
                        ye_v[pl.ds(wbase + k * ln, ln)] = ev[k]
                        yo_v[pl.ds(wbase + k * ln, ln)] = od[k]
                    return c2
                lax.fori_loop(0, d // (cb_n * ln), colblock, 0)

            _sc_gather_pipeline(tab_hbm, idx_v, (rows0, rows1), (sem0, sem1), n_sel // ch, compute)

            def weave(c, c2):
                plsc.store_scatter(y_v, [c * 2 * ln + 2 * lane], ye_v[pl.ds(c * ln, ln)])
                plsc.store_scatter(y_v, [c * 2 * ln + 2 * lane + 1], yo_v[pl.ds(c * ln, ln)])
                return c2
            lax.fori_loop(0, words // ln, weave, 0)
            pltpu.sync_copy(y_v, out_hbm.at[loc])
            return carry

        lax.fori_loop(0, per_w, token, 0)

    return pl.kernel(
        body, out_type=jax.ShapeDtypeStruct((n_tok, d), F32), mesh=_sc_mesh(),
        scratch_types=[pltpu.VMEM((n_sel,), F32), pltpu.VMEM((n_sel,), jnp.int32),
                       pltpu.VMEM((ch, words), jnp.int32), pltpu.VMEM((ch, words), jnp.int32),
                       pltpu.VMEM((n_sel, ln), F32), pltpu.VMEM((d,), F32),
                       pltpu.VMEM((words,), F32), pltpu.VMEM((words,), F32),
                       pltpu.SemaphoreType.DMA, pltpu.SemaphoreType.DMA],
        compiler_params=_sc_params(), name="sc_mix")(act, idx_tm, tab)


def _act_kernel(d_ref, w_ref, after_ref, o_ref):
    del after_ref
    o_ref[...] = _gelu(d_ref[...]) * w_ref[...]


def _activation(dots, w_tm, after):
    n, k = dots.shape
    tm = min(ROW_TILE * 4, n)
    spec = pl.BlockSpec((tm, k), lambda i: (i, 0))
    anchor = pl.BlockSpec((SUBLANES, after.shape[1]), lambda i: (0, 0))
    return pl.pallas_call(_act_kernel, grid=(n // tm,), in_specs=[spec, spec, anchor], out_specs=spec,
                          out_shape=jax.ShapeDtypeStruct((n, k), F32),
                          compiler_params=_cparams(("arbitrary",)), name="peer_act")(dots, w_tm, after)


def _finish_kernel(x_ref, y_ref, mod_ref, fg_ref, o_ref):
    o_ref[...] = _rms(x_ref[...] + mod_ref[0] * y_ref[...], fg_ref[...])


def _finish(x1, y, gate_mod, fg, seq_len, t0):
    n, d = y.shape
    tm = min(ROW_TILE, seq_len)
    per_seq = seq_len // tm
    first = t0 // tm
    return pl.pallas_call(
        _finish_kernel,
        grid=(n // tm,),
        in_specs=[pl.BlockSpec((tm, d), lambda i: (first + i, 0)),
                  pl.BlockSpec((tm, d), lambda i: (i, 0)),
                  pl.BlockSpec((1, 1, d), lambda i: ((first + i) // per_seq, 0, 0)),
                  pl.BlockSpec((1, d), lambda i: (0, 0))],
        out_specs=pl.BlockSpec((tm, d), lambda i: (i, 0)),
        out_shape=jax.ShapeDtypeStruct((n, d), F32),
        compiler_params=_cparams(("arbitrary",)),
        name="peer_finish",
    )(x1, y, gate_mod, fg)


def _block_diag(w):
    g, i, j = w.shape
    eye = jnp.eye(g, dtype=w.dtype)
    return (eye[:, None, :, None] * w[:, :, None, :]).reshape(g * i, g * j)


def kernel(x, c, ctx, c_ctx, ada_w, ada_b, norm1_g, w_in, conv_w, conv_b, rg_w_a, rg_b_a, rg_w_x, rg_b_x, rg_lambda, gla_w_g, gla_b_g, gla_norm_g, w_out, norm2_g, peer_w_q, peer_keys, peer_u, peer_v, final_norm_g):
    bsz, t_lat, d = x.shape
    t_ctx = ctx.shape[1]
    assert ada_w.shape[0] == 1, "single-layer block"
    n = bsz * t_lat
    x2 = x.reshape(n, d)
    ctx2 = ctx.reshape(bsz * t_ctx, d)
    pos = _sincos_2d(t_lat, d)

    pad_rows = (-(bsz + 1)) % SUBLANES
    cc = jnp.concatenate([c, c_ctx[None, :], jnp.zeros((pad_rows, d), F32)], axis=0)
    mod = _adaln(cc, ada_w[0], ada_b[0][None, :])
    mod_l = mod[:bsz].reshape(bsz, N_MOD, 1, d)
    mod_c = mod[bsz].reshape(N_MOD, 1, 1, d)

    w_in_p = jnp.pad(w_in[0], ((0, 0), (0, IN_COLS_PAD - w_in.shape[2]))).astype(BF16)
    g1 = norm1_g[0][None, :]
    h_l = _inproj(x2, pos, mod_l[:, 0], mod_l[:, 1], g1, w_in_p, t_lat)
    h_c = _inproj(ctx2, None, mod_c[0], mod_c[1], g1, w_in_p, t_ctx)

    wg_rg = jnp.concatenate([_block_diag(rg_w_a[0, 0]), _block_diag(rg_w_x[0, 0]),
                             _block_diag(rg_w_a[0, 1]), _block_diag(rg_w_x[0, 1])], axis=1).astype(BF16)
    bg_rg = jnp.concatenate([rg_b_a[0, 0], rg_b_x[0, 0], rg_b_a[0, 1], rg_b_x[0, 1]])[None, :]
    y_rg = _rglru(h_l, h_c, conv_w[0], conv_b[0][None, :], wg_rg, bg_rg, rg_lambda[0], bsz, t_lat, t_ctx)

    wg_gla = jnp.zeros((2, LANES, QK_WIDTH), F32)
    wg_gla = wg_gla.at[0, 0:GLA_RANK].set(gla_w_g[0, 0]).at[1, GLA_RANK:2 * GLA_RANK].set(gla_w_g[0, 1])
    o_gla = _gla(h_l, h_c, wg_gla, gla_b_g[0][:, None, :], bsz, t_lat, t_ctx)

    x1 = _merge(y_rg, h_l, o_gla, x2, pos, mod_l[:, 2], gla_norm_g[0][None, :], w_out[0].astype(BF16), t_lat)

    half = PEER_DQ // 2
    wq = peer_w_q[0].reshape(d, PEER_HEADS, 2, half).transpose(0, 2, 1, 3).reshape(d, 2 * PEER_HEADS * half).astype(BF16)
    kt = peer_keys[0].transpose(1, 2, 0, 3)
    wk = (kt[:, :, :, None, :] * jnp.eye(PEER_HEADS, dtype=F32)[None, None, :, :, None])
    wk = wk.reshape(2, N_KEYS * PEER_HEADS, PEER_HEADS * half).astype(BF16)
    u, eidx, wts = _select(x1, mod_l[:, 3], mod_l[:, 4], norm2_g[0][None, :], wq, wk, t_lat)
    tt = GATHER_TOKENS
    idx_tm = eidx.T
    idx3 = idx_tm.reshape(n // tt, 1, tt * N_SEL)
    u_tiles = peer_u[0].reshape(peer_u.shape[1], d // LANES, LANES)
    v_tiles = peer_v[0].reshape(peer_v.shape[1], d // LANES, LANES)
    uv_tab = jnp.concatenate([u_tiles, v_tiles], axis=1).astype(BF16)
    fg = final_norm_g[None, :]
    sc_seqs = SC_SEQUENCES if (bsz > SC_SEQUENCES and t_lat % (4 * ROW_TILE) == 0) else 0
    n_tc = (bsz - sc_seqs) * t_lat
    gate = mod_l[:, 5]
    if sc_seqs:
        n_a = min(TC_FIRST_SEQUENCES, bsz - sc_seqs - 1) * t_lat
        n_sc = n - n_tc
        out_a = _peer(idx3, u, wts, x1, gate, fg, uv_tab, t_lat, 0, n_a)
        dots = _sc_dots(u, idx_tm, _pack_bf16_rows(peer_u[0]), n_tc, n_sc)
        act = _activation(dots, wts[:, n_tc:].T, out_a)
        out_b = _peer(idx3, u, wts, x1, gate, fg, uv_tab, t_lat, n_a, n_tc - n_a)
        y_sc = _sc_mix(act, idx_tm, _pack_bf16_rows(peer_v[0]), n_tc, n_sc)
        out = jnp.concatenate([out_a, out_b, _finish(x1, y_sc, gate, fg, t_lat, n_tc)], axis=0)
    else:
        out = _peer(idx3, u, wts, x1, gate, fg, uv_tab, t_lat, 0, n)
    return out.reshape(bsz, t_lat, d)
```

```python
import functools
import math

import jax
import jax.numpy as jnp
from jax import lax
from jax.experimental import pallas as pl
from jax.experimental.pallas import tpu as pltpu

F32 = jnp.float32
BF16 = jnp.bfloat16

GRID_W = 64
POS_THETA = 10000.0
RMS_EPS = 1e-6
N_MOD = 6
RG_WIDTH = 512
RG_BLOCKS = 8
RG_C = 8.0
CONV_W = 4
GLA_HEADS = 4
GLA_DK = 64
GLA_DV = 128
GLA_WIDTH = GLA_HEADS * GLA_DV
GLA_RANK = 16
GLA_GATE_NORM = 16.0
GLA_CHUNK = 64
QK_WIDTH = GLA_HEADS * GLA_DK
N_KEYS = 128
PEER_HEADS = 8
PEER_DQ = 256
PEER_TOPK = 16
N_SEL = PEER_HEADS * PEER_TOPK

LANES = 128
SUBLANES = 8
VMEM_LIMIT = 56 * 1024 * 1024

COL_RX, COL_RGATE, COL_Q, COL_K, COL_V, COL_R, COL_LR = 0, 512, 1024, 1280, 1536, 2048, 2560
IN_COLS_PAD = 2688

SCAN_BLOCK = 128
HALO = 8
ROW_TILE = 512
SEL_TILE = 256
GATHER_TOKENS = 8


def _cparams(sem, vmem=VMEM_LIMIT):
    return pltpu.CompilerParams(dimension_semantics=sem, vmem_limit_bytes=vmem)


def _sincos_2d(n_tokens, dim):
    rows = n_tokens // GRID_W
    r, col = jnp.meshgrid(jnp.arange(rows, dtype=F32), jnp.arange(GRID_W, dtype=F32), indexing="ij")
    quarter = dim // 4
    omega = POS_THETA ** (-jnp.arange(quarter, dtype=F32) / quarter)

    def axis_embed(p):
        ang = p.reshape(-1)[:, None] * omega[None, :]
        return jnp.concatenate([jnp.sin(ang), jnp.cos(ang)], axis=-1)

    return jnp.concatenate([axis_embed(r), axis_embed(col)], axis=-1)


def _gelu(x):
    return 0.5 * x * (1.0 + lax.erf(x * (1.0 / math.sqrt(2.0))))


def _rms(x, g):
    ms = jnp.mean(x * x, axis=-1, keepdims=True)
    return x * lax.rsqrt(ms + RMS_EPS) * g


def _adaln_kernel(c_ref, w_ref, b_ref, o_ref):
    a = jax.nn.silu(c_ref[...]).astype(BF16)
    o_ref[...] = jnp.dot(a, w_ref[...].astype(BF16), preferred_element_type=F32) + b_ref[...]


def _adaln(cc, w, b):
    rows, d = cc.shape
    n = w.shape[1]
    tn = 1536
    return pl.pallas_call(
        _adaln_kernel,
        grid=(n // tn,),
        in_specs=[pl.BlockSpec((rows, d), lambda j: (0, 0)),
                  pl.BlockSpec((d, tn), lambda j: (0, j)),
                  pl.BlockSpec((1, tn), lambda j: (0, j))],
        out_specs=pl.BlockSpec((rows, tn), lambda j: (0, j)),
        out_shape=jax.ShapeDtypeStruct((rows, n), F32),
        compiler_params=_cparams(("arbitrary",)),
        name="adaln",
    )(cc, w, b)


def _inproj_kernel(*refs, has_pos):
    if has_pos:
        x_ref, pos_ref, shift_ref, scale_ref, g_ref, w_ref, o_ref = refs
        x = x_ref[...] + pos_ref[...]
    else:
        x_ref, shift_ref, scale_ref, g_ref, w_ref, o_ref = refs
        x = x_ref[...]
    y = _rms(x, g_ref[...]) * (1.0 + scale_ref[0]) + shift_ref[0]
    o_ref[...] = jnp.dot(y.astype(BF16), w_ref[...], preferred_element_type=F32)


def _inproj(x2, pos, shift, scale, g, w, seq_len):
    n, d = x2.shape
    cols = w.shape[1]
    tm = min(ROW_TILE, seq_len)
    per_seq = seq_len // tm
    shared = shift.shape[0] == 1
    mod_map = (lambda i: (0, 0, 0)) if shared else (lambda i: (i // per_seq, 0, 0))
    in_specs = [pl.BlockSpec((tm, d), lambda i: (i, 0))]
    args = [x2]
    if pos is not None:
        in_specs.append(pl.BlockSpec((tm, d), lambda i: (i % per_seq, 0)))
        args.append(pos)
    in_specs += [pl.BlockSpec((1, 1, d), mod_map), pl.BlockSpec((1, 1, d), mod_map),
                 pl.BlockSpec((1, d), lambda i: (0, 0)), pl.BlockSpec((d, cols), lambda i: (0, 0))]
    args += [shift, scale, g, w]
    return pl.pallas_call(
        functools.partial(_inproj_kernel, has_pos=pos is not None),
        grid=(n // tm,),
        in_specs=in_specs,
        out_specs=pl.BlockSpec((tm, cols), lambda i: (i, 0)),
        out_shape=jax.ShapeDtypeStruct((n, cols), F32),
        compiler_params=_cparams(("arbitrary",)),
        name="inproj",
    )(*args)


def _rglru_kernel(rxl_ref, rxc_ref, cw_ref, cb_ref, wg_ref, bg_ref, lam_ref, y_ref, xpl_ref, xpc_ref, *, t_lat, t_ctx):
    tb = SCAN_BLOCK
    w = RG_WIDTH
    zeros_halo = jnp.zeros((HALO, w), F32)
    for xp_ref, src_ref, t in ((xpl_ref, rxl_ref, t_lat), (xpc_ref, rxc_ref, t_ctx)):
        xp_ref[0:HALO, :] = zeros_halo
        xp_ref[HALO:HALO + t, :] = src_ref[...]
        xp_ref[HALO + t:HALO + t + HALO, :] = zeros_halo

    cw = cw_ref[...]
    cb = cb_ref[...]
    row = lax.broadcasted_iota(jnp.int32, (tb, w), 0)

    def block(xp_ref, r0, d, h_in):
        ext = xp_ref[pl.ds(r0, tb + 2 * HALO), :]
        xc = (ext[HALO - 2:HALO - 2 + tb] * cw[0:1] + ext[HALO - 1:HALO - 1 + tb] * cw[1:2]
              + ext[HALO:HALO + tb] * cw[2:3] + ext[HALO + 1:HALO + 1 + tb] * cw[3:4] + cb)
        g = jnp.dot(xc.astype(BF16), wg_ref[:, 2 * w * d:2 * w * (d + 1)], preferred_element_type=F32)
        g = g + bg_ref[:, 2 * w * d:2 * w * (d + 1)]
        gate_r = jax.nn.sigmoid(g[:, :w])
        gate_i = jax.nn.sigmoid(g[:, w:])
        log_a = (-RG_C * jax.nn.softplus(-lam_ref[d:d + 1, :])) * gate_r
        a = jnp.exp(log_a)
        th = jnp.tanh(log_a)
        bv = jnp.sqrt(-2.0 * th / (1.0 - th)) * gate_i * xc
        s = 1
        while s < tb:
            if d == 0:
                valid = row >= s
                a_p = jnp.where(valid, pltpu.roll(a, s, 0), 1.0)
                b_p = jnp.where(valid, pltpu.roll(bv, s, 0), 0.0)
            else:
                valid = row < tb - s
                a_p = jnp.where(valid, pltpu.roll(a, tb - s, 0), 1.0)
                b_p = jnp.where(valid, pltpu.roll(bv, tb - s, 0), 0.0)
            bv = a * b_p + bv
            a = a * a_p
            s *= 2
        h = a * h_in + bv
        h_out = h[tb - 1:tb] if d == 0 else h[0:1]
        return h, h_out

    n_lat = t_lat // tb
    n_ctx = t_ctx // tb
    h0 = jnp.zeros((1, w), F32)

    y_ref[...] = jnp.zeros(y_ref.shape, F32)

    def both_ctx(i, hs):
        return (block(xpc_ref, pl.multiple_of(i * tb, tb), 0, hs[0])[1],
                block(xpc_ref, pl.multiple_of((n_ctx - 1 - i) * tb, tb), 1, hs[1])[1])

    def both_lat(i, hs):
        rf = pl.multiple_of(i * tb, tb)
        hf, hf_out = block(xpl_ref, rf, 0, hs[0])
        y_ref[pl.ds(rf, tb), :] = y_ref[pl.ds(rf, tb), :] + hf
        rb = pl.multiple_of((n_lat - 1 - i) * tb, tb)
        hb, hb_out = block(xpl_ref, rb, 1, hs[1])
        y_ref[pl.ds(rb, tb), :] = y_ref[pl.ds(rb, tb), :] + hb
        return hf_out, hb_out

    lax.fori_loop(0, n_lat, both_lat, lax.fori_loop(0, n_ctx, both_ctx, (h0, h0)))


def _rglru(h_l, h_c, conv_w, conv_b, wg, bg, lam, batch, t_lat, t_ctx):
    w = RG_WIDTH
    kern = functools.partial(_rglru_kernel, t_lat=t_lat, t_ctx=t_ctx)
    full = lambda shape: pl.BlockSpec(shape, lambda b: tuple(0 for _ in shape))
    return pl.pallas_call(
        kern,
        grid=(batch,),
        in_specs=[pl.BlockSpec((t_lat, w), lambda b: (b, COL_RX // w)),
                  pl.BlockSpec((t_ctx, w), lambda b: (b, COL_RX // w)),
                  full((CONV_W, w)), full((1, w)), full((w, 4 * w)), full((1, 4 * w)), full((2, w))],
        out_specs=pl.BlockSpec((t_lat, w), lambda b: (b, 0)),
        out_shape=jax.ShapeDtypeStruct((batch * t_lat, w), F32),
        scratch_shapes=[pltpu.VMEM((t_lat + 2 * HALO, w), F32), pltpu.VMEM((t_ctx + 2 * HALO, w), F32)],
        compiler_params=_cparams(("arbitrary",)),
        name="rglru",
    )(h_l, h_c, conv_w, conv_b, wg, bg, lam)


def _gla_kernel(ql_ref, kl_ref, vl_ref, lrl_ref, kc_ref, vc_ref, lrc_ref, wg_ref, bg_ref, o_ref, *, t_lat, t_ctx):
    c = GLA_CHUNK
    qk = QK_WIDTH
    scale = GLA_DK ** -0.5
    row = lax.broadcasted_iota(jnp.int32, (c, qk), 0)
    ri = lax.broadcasted_iota(jnp.int32, (c, c), 0)
    ci = lax.broadcasted_iota(jnp.int32, (c, c), 1)

    def decays(lr_ref, r0, d):
        z = jnp.dot(lr_ref[pl.ds(r0, c), :].astype(BF16), wg_ref[d].astype(BF16), preferred_element_type=F32) + bg_ref[d]
        b = jax.nn.log_sigmoid(z) * (1.0 / GLA_GATE_NORM)
        s = 1
        while s < c:
            if d == 0:
                b = b + jnp.where(row >= s, pltpu.roll(b, s, 0), 0.0)
            else:
                b = b + jnp.where(row < c - s, pltpu.roll(b, c - s, 0), 0.0)
            s *= 2
        b_last = b[c - 1:c] if d == 0 else b[0:1]
        return b, b_last

    def state_update(st, k, v, b, b_last):
        k_dec = (k * jnp.exp(b_last - b)).astype(BF16)
        vb = v.astype(BF16)
        parts = []
        for h in range(GLA_HEADS):
            parts.append(lax.dot_general(vb[:, GLA_DV * h:GLA_DV * (h + 1)], k_dec[:, GLA_DK * h:GLA_DK * (h + 1)],
                                         (((0,), (0,)), ((), ())), preferred_element_type=F32))
        return st * jnp.exp(b_last) + jnp.concatenate(parts, axis=1)

    def ctx_chunk(r0, d, st):
        b, b_last = decays(lrc_ref, r0, d)
        return state_update(st, kc_ref[pl.ds(r0, c), :], vc_ref[pl.ds(r0, c), :], b, b_last)

    def lat_chunk(r0, d, st):
        b, b_last = decays(lrl_ref, r0, d)
        q = ql_ref[pl.ds(r0, c), :] * scale
        k = kl_ref[pl.ds(r0, c), :]
        v = vl_ref[pl.ds(r0, c), :]
        q_dec = (q * jnp.exp(b)).astype(BF16)
        k_inv = (k * jnp.exp(-b)).astype(BF16)
        vb = v.astype(BF16)
        stb = st.astype(BF16)
        keep = (ci <= ri) if d == 0 else (ci >= ri)
        outs = []
        for h in range(GLA_HEADS):
            ks = slice(GLA_DK * h, GLA_DK * (h + 1))
            vs = slice(GLA_DV * h, GLA_DV * (h + 1))
            sc = lax.dot_general(q_dec[:, ks], k_inv[:, ks], (((1,), (1,)), ((), ())), preferred_element_type=F32)
            sc = jnp.where(keep, sc, 0.0).astype(BF16)
            o_h = jnp.dot(sc, vb[:, vs], preferred_element_type=F32)
            o_h = o_h + lax.dot_general(q_dec[:, ks], stb[:, ks], (((1,), (1,)), ((), ())), preferred_element_type=F32)
            outs.append(o_h)
        o = jnp.concatenate(outs, axis=1)
        o_ref[pl.ds(r0, c), :] = o_ref[pl.ds(r0, c), :] + o
        return state_update(st, k, v, b, b_last)

    n_lat = t_lat // c
    n_ctx = t_ctx // c
    st0 = jnp.zeros((GLA_DV, qk), F32)
    o_ref[...] = jnp.zeros(o_ref.shape, F32)

    def both_ctx(i, carry):
        return (ctx_chunk(pl.multiple_of(i * c, c), 0, carry[0]),
                ctx_chunk(pl.multiple_of((n_ctx - 1 - i) * c, c), 1, carry[1]))

    def both_lat(i, carry):
        return (lat_chunk(pl.multiple_of(i * c, c), 0, carry[0]),
                lat_chunk(pl.multiple_of((n_lat - 1 - i) * c, c), 1, carry[1]))

    lax.fori_loop(0, n_lat, both_lat, lax.fori_loop(0, n_ctx, both_ctx, (st0, st0)))


def _gla(h_l, h_c, wg_pad, bg, batch, t_lat, t_ctx):
    kern = functools.partial(_gla_kernel, t_lat=t_lat, t_ctx=t_ctx)
    qk = QK_WIDTH
    col = lambda t, width, off: pl.BlockSpec((t, width), lambda b: (b, off // width))
    return pl.pallas_call(
        kern,
        grid=(batch,),
        in_specs=[col(t_lat, qk, COL_Q), col(t_lat, qk, COL_K), col(t_lat, GLA_WIDTH, COL_V), col(t_lat, LANES, COL_LR),
                  col(t_ctx, qk, COL_K), col(t_ctx, GLA_WIDTH, COL_V), col(t_ctx, LANES, COL_LR),
                  pl.BlockSpec((2, LANES, qk), lambda b: (0, 0, 0)), pl.BlockSpec((2, 1, qk), lambda b: (0, 0, 0))],
        out_specs=pl.BlockSpec((t_lat, GLA_WIDTH), lambda b: (b, 0)),
        out_shape=jax.ShapeDtypeStruct((batch * t_lat, GLA_WIDTH), F32),
        compiler_params=_cparams(("arbitrary",)),
        name="gla",
    )(h_l, h_l, h_l, h_l, h_c, h_c, h_c, wg_pad, bg)


def _merge_kernel(y_ref, gate_ref, o_ref, r_ref, x_ref, pos_ref, mod_ref, ng_ref, w_ref, out_ref):
    rg = y_ref[...] * _gelu(gate_ref[...])
    og = o_ref[...]
    heads = [_rms(og[:, GLA_DV * h:GLA_DV * (h + 1)], ng_ref[...]) for h in range(GLA_HEADS)]
    gla = jnp.concatenate(heads, axis=1) * jax.nn.silu(r_ref[...])
    m = jnp.dot(rg.astype(BF16), w_ref[0:RG_WIDTH, :], preferred_element_type=F32)
    m = m + jnp.dot(gla.astype(BF16), w_ref[RG_WIDTH:, :], preferred_element_type=F32)
    out_ref[...] = x_ref[...] + pos_ref[...] + mod_ref[0] * m


def _merge(y_rg, h_l, o_gla, x2, pos, gate_mod, ng, w_out, seq_len):
    n, d = x2.shape
    tm = min(ROW_TILE, seq_len)
    per_seq = seq_len // tm
    w = RG_WIDTH
    return pl.pallas_call(
        _merge_kernel,
        grid=(n // tm,),
        in_specs=[pl.BlockSpec((tm, w), lambda i: (i, 0)),
                  pl.BlockSpec((tm, w), lambda i: (i, COL_RGATE // w)),
                  pl.BlockSpec((tm, GLA_WIDTH), lambda i: (i, 0)),
                  pl.BlockSpec((tm, GLA_WIDTH), lambda i: (i, COL_R // GLA_WIDTH)),
                  pl.BlockSpec((tm, d), lambda i: (i, 0)),
                  pl.BlockSpec((tm, d), lambda i: (i % per_seq, 0)),
                  pl.BlockSpec((1, 1, d), lambda i: (i // per_seq, 0, 0)),
                  pl.BlockSpec((1, GLA_DV), lambda i: (0, 0)),
                  pl.BlockSpec((w + GLA_WIDTH, d), lambda i: (0, 0))],
        out_specs=pl.BlockSpec((tm, d), lambda i: (i, 0)),
        out_shape=jax.ShapeDtypeStruct((n, d), F32),
        compiler_params=_cparams(("arbitrary",)),
        name="merge",
    )(y_rg, h_l, o_gla, h_l, x2, pos, gate_mod, ng, w_out)


def _sort_network(n):
    pairs = []
    p = 1
    while p < n:
        k = p
        while k >= 1:
            for j in range(k % p, n - k, 2 * k):
                for i in range(min(k, n - j - k)):
                    if (i + j) // (2 * p) == (i + j + k) // (2 * p):
                        pairs.append((i + j, i + j + k))
            k //= 2
        p *= 2
    return pairs


def _bitonic_network(n):
    pairs = []
    s = n // 2
    while s >= 1:
        pairs += [(i, i + s) for i in range(n) if (i & s) == 0]
        s //= 2
    return pairs


SORT16 = _sort_network(PEER_TOPK)
BITONIC16 = _bitonic_network(PEER_TOPK)
TAG_BITS = 14
PAD_TAG = 1 << 30


def _beats(a, b):
    return (a[0] > b[0]) | ((a[0] == b[0]) & (a[1] < b[1]))


def _exchange(w, i, j):
    a, b = w[i], w[j]
    c = _beats(a, b)
    w[i] = (jnp.maximum(a[0], b[0]), jnp.where(c, a[1], b[1]))
    w[j] = (jnp.minimum(a[0], b[0]), jnp.where(c, b[1], a[1]))


def _sorted16(w):
    w = list(w)
    for i, j in SORT16:
        _exchange(w, i, j)
    return w


def _merge_top16(a, b):
    w = []
    for i in range(PEER_TOPK):
        x, y = a[i], b[PEER_TOPK - 1 - i]
        c = _beats(x, y)
        w.append((jnp.maximum(x[0], y[0]), jnp.where(c, x[1], y[1])))
    for i, j in BITONIC16:
        _exchange(w, i, j)
    return w


def _top16_of(elems):
    lists = [_sorted16(elems[i:i + PEER_TOPK]) for i in range(0, len(elems), PEER_TOPK)]
    while len(lists) > 1:
        nxt = [_merge_top16(lists[i], lists[i + 1]) for i in range(0, len(lists) - 1, 2)]
        if len(lists) % 2:
            nxt.append(lists[-1])
        lists = nxt
    return lists[0]


def _product_key_top16(scores1, scores2):
    shape = scores1[0].shape
    tops = []
    for scores in (scores1, scores2):
        tops.append(_top16_of([(s, jnp.full(shape, k, jnp.int32)) for k, s in enumerate(scores)]))
    s1, s2 = tops
    cand = []
    for a in range(PEER_TOPK):
        for b in range(PEER_TOPK // (a + 1)):
            tag = (s1[a][1] * N_KEYS + s2[b][1]) + ((a * PEER_TOPK + b) << TAG_BITS)
            cand.append((s1[a][0] + s2[b][0], tag))
    pad = (jnp.full(shape, -jnp.inf, F32), jnp.full(shape, PAD_TAG, jnp.int32))
    cand += [pad] * ((-len(cand)) % PEER_TOPK)
    top = _top16_of(cand)
    ex = [jnp.exp(t[0] - top[0][0]) for t in top]
    tot = ex[0]
    for e in ex[1:]:
        tot = tot + e
    inv = 1.0 / tot
    return [t[1] & ((1 << TAG_BITS) - 1) for t in top], [e * inv for e in ex]


def _select_kernel(x_ref, shift_ref, scale_ref, g_ref, wq_ref, wk_ref, u_ref, e_ref, w_ref, s_scr):
    u = _rms(x_ref[...], g_ref[...]) * (1.0 + scale_ref[0]) + shift_ref[0]
    u_ref[...] = u
    q = jnp.dot(u.astype(BF16), wq_ref[...], preferred_element_type=F32).astype(BF16)
    hd = q.shape[1] // 2
    for p in range(2):
        s_scr[p] = lax.dot_general(wk_ref[p], q[:, p * hd:(p + 1) * hd], (((1,), (1,)), ((), ())),
                                   preferred_element_type=F32)
    h8 = PEER_HEADS

    def lane_group(gi, carry):
        lanes = pl.ds(pl.multiple_of(gi * LANES, LANES), LANES)
        slabs = [[s_scr[p, h8 * k:h8 * (k + 1), lanes] for k in range(N_KEYS)] for p in range(2)]
        experts, weights = _product_key_top16(slabs[0], slabs[1])
        for k in range(PEER_TOPK):
            e_ref[h8 * k:h8 * (k + 1), lanes] = experts[k]
            w_ref[h8 * k:h8 * (k + 1), lanes] = weights[k]
        return carry

    lax.fori_loop(0, x_ref.shape[0] // LANES, lane_group, 0)


def _select(x1, shift, scale, g, wq, wk, seq_len):
    n, d = x1.shape
    tq = min(SEL_TILE, seq_len)
    per_seq = seq_len // tq
    mod_map = lambda i: (i // per_seq, 0, 0)
    return pl.pallas_call(
        _select_kernel,
        grid=(n // tq,),
        in_specs=[pl.BlockSpec((tq, d), lambda i: (i, 0)),
                  pl.BlockSpec((1, 1, d), mod_map), pl.BlockSpec((1, 1, d), mod_map),
                  pl.BlockSpec((1, d), lambda i: (0, 0)),
                  pl.BlockSpec(wq.shape, lambda i: (0, 0)),
                  pl.BlockSpec(wk.shape, lambda i: (0, 0, 0))],
        out_specs=[pl.BlockSpec((tq, d), lambda i: (i, 0)),
                   pl.BlockSpec((N_SEL, tq), lambda i: (0, i)),
                   pl.BlockSpec((N_SEL, tq), lambda i: (0, i))],
        out_shape=[jax.ShapeDtypeStruct((n, d), F32),
                   jax.ShapeDtypeStruct((N_SEL, n), jnp.int32),
                   jax.ShapeDtypeStruct((N_SEL, n), F32)],
        scratch_shapes=[pltpu.VMEM((2, N_KEYS * PEER_HEADS, tq), F32)],
        compiler_params=_cparams(("arbitrary",)),
        name="peer_select",
    )(x1, shift, scale, g, wq, wk)


def _fold_sublanes(p):
    sub = lax.broadcasted_iota(jnp.int32, p.shape[2:], 0)
    half = SUBLANES // 2
    while half >= 1:
        k = p.shape[1] // 2
        a, b = p[:, :k], p[:, k:]
        low = (sub & half) == 0
        p = jnp.where(low, a, b) + jnp.where(low, pltpu.roll(a, SUBLANES - half, 2), pltpu.roll(b, half, 2))
        half //= 2
    return p[:, 0]


def _peer_kernel(idx0_ref, idxn_ref, u_ref, w_ref, x_ref, mod_ref, fg_ref, tab_hbm, out_ref, gbuf, xs, yacc, wsc, sem):
    tt = GATHER_TOKENS
    rows = tt * N_SEL
    d = x_ref.shape[1]
    s8 = SUBLANES
    groups = N_SEL // s8
    i = pl.program_id(0)
    n = pl.num_programs(0)

    def issue(idx_ref, slot, r_lo, r_hi):
        for r in range(r_lo, r_hi):
            e = idx_ref[0, 0, r]
            pltpu.make_async_copy(tab_hbm.at[pl.ds(e, 1)], gbuf.at[slot, pl.ds(r, 1)], sem.at[slot]).start(priority=r % 2)

    def wait_slot(slot):
        pltpu.make_async_copy(tab_hbm.at[pl.ds(0, rows)], gbuf.at[slot], sem.at[slot]).wait()

    @pl.when(i == 0)
    def _():
        issue(idx0_ref, 0, 0, rows)

    def step(cur, nxt):
        for t in range(tt):
            for s in range(s8):
                xs[t, s:s + 1, :] = u_ref[t:t + 1, LANES * s:LANES * (s + 1)]
        off = (i % (LANES // tt)) * tt
        wsc[...] = pltpu.roll(w_ref[...], (LANES - off) % LANES, 1)
        wait_slot(cur)
        half = s8 // 2
        for t in range(tt):
            xt = xs[t]
            dots = []
            for jb in range(groups):
                r0 = t * N_SEL + jb * s8
                issue(idxn_ref, nxt, r0, r0 + half)
                p = gbuf[cur, r0:r0 + s8, 0:s8, :].astype(F32) * xt
                dots.append(jnp.sum(_fold_sublanes(p[None]), axis=-1, keepdims=True)[0])
            wcol = wsc[:, t:t + 1]
            y = jnp.zeros((s8, LANES), F32)
            for jb in range(groups):
                r0 = t * N_SEL + jb * s8
                issue(idxn_ref, nxt, r0 + half, r0 + s8)
                act = _gelu(dots[jb]) * wcol[jb * s8:(jb + 1) * s8, :]
                v = gbuf[cur, r0:r0 + s8, s8:2 * s8, :].astype(F32)
                for jj in range(s8):
                    y = y + v[jj] * act[jj:jj + 1, :]
            for s in range(s8):
                yacc[t:t + 1, LANES * s:LANES * (s + 1)] = y[s:s + 1, :]

    @pl.when(i % 2 == 0)
    def _():
        step(0, 1)

    @pl.when(i % 2 == 1)
    def _():
        step(1, 0)

    @pl.when(i == n - 1)
    def _():
        wait_slot((i + 1) % 2)

    x2 = x_ref[...] + mod_ref[0] * yacc[...]
    out_ref[...] = _rms(x2, fg_ref[...])


def _drop_operand(kernel, position):
    def entry(*refs):
        return kernel(*(refs[:position] + refs[position + 1:]))
    return entry


def _peer(idx3, u, wts, x1, gate_mod, fg, uv_tab, seq_len, t0, n, out_prev=None):
    d = x1.shape[1]
    tt = GATHER_TOKENS
    n_tiles = n // tt
    off = t0 // tt
    per_seq = seq_len // tt
    rows = tt * N_SEL
    smem_spec = lambda m: pl.BlockSpec((1, 1, rows), m, memory_space=pltpu.SMEM)
    return pl.pallas_call(
        _drop_operand(_peer_kernel, 8) if out_prev is not None else _peer_kernel,
        grid=(n_tiles,),
        in_specs=[smem_spec(lambda i: (off, 0, 0)),
                  smem_spec(lambda i: (off + jnp.minimum(i + 1, n_tiles - 1), 0, 0)),
                  pl.BlockSpec((tt, d), lambda i: (off + i, 0)),
                  pl.BlockSpec((N_SEL, LANES), lambda i: (0, (off + i) // (LANES // tt))),
                  pl.BlockSpec((tt, d), lambda i: (off + i, 0)),
                  pl.BlockSpec((1, 1, d), lambda i: ((off + i) // per_seq, 0, 0)),
                  pl.BlockSpec((1, d), lambda i: (0, 0)),
                  pl.BlockSpec(memory_space=pl.ANY)]
                 + ([pl.BlockSpec(memory_space=pl.ANY)] if out_prev is not None else []),
        out_specs=pl.BlockSpec((tt, d), lambda i: (off + i, 0)),
        out_shape=jax.ShapeDtypeStruct(x1.shape, F32),
        input_output_aliases={8: 0} if out_prev is not None else {},
        scratch_shapes=[pltpu.VMEM((2, rows) + uv_tab.shape[1:], uv_tab.dtype), pltpu.VMEM((tt, SUBLANES, LANES), F32),
                        pltpu.VMEM((tt, d), F32), pltpu.VMEM((N_SEL, LANES), F32), pltpu.SemaphoreType.DMA((2,))],
        compiler_params=_cparams(("arbitrary",)),
        name="peer_gather",
    )(idx3, idx3, u, wts, x1, gate_mod, fg, uv_tab, *([out_prev] if out_prev is not None else []))


SC_LANES = 16
SC_CORES = 2
SC_SUBCORES = 16
SC_WORKERS = SC_CORES * SC_SUBCORES
SC_CHUNK = 32
SC_BLOCK = 4
SC_SEQUENCES = 7
TC_FIRST_SEQUENCES = 5


def _sc_params():
    import dataclasses
    cp = pltpu.CompilerParams()
    if "needs_layout_passes" in pltpu.CompilerParams.__dataclass_fields__:
        cp = dataclasses.replace(cp, needs_layout_passes=False)
    return cp


def _bf16_low(w):
    return lax.bitcast_convert_type(w << 16, F32)


def _bf16_high(w):
    return lax.bitcast_convert_type(w & (-65536), F32)


def _pack_bf16_rows(tab):
    rows, d = tab.shape
    return lax.bitcast_convert_type(tab.astype(BF16).reshape(rows, d // 2, 2), jnp.int32)


def _sc_mesh():
    from jax.experimental.pallas import tpu_sc as plsc
    return plsc.VectorSubcoreMesh(core_axis_name="c", subcore_axis_name="s", num_cores=SC_CORES, num_subcores=SC_SUBCORES)


def _sc_gather_pipeline(tab_hbm, idx_v, bufs, sems, n_chunks, compute):
    def copy(h, b):
        return pltpu.make_async_copy(tab_hbm.at[idx_v.at[pl.ds(h * SC_CHUNK, SC_CHUNK)]], bufs[b], sems[b])

    copy(0, 0).start()

    def pair(hp, carry):
        h0 = 2 * hp
        copy(h0 + 1, 1).start()
        copy(h0, 0).wait()
        compute(h0, bufs[0])

        @pl.when(hp + 1 < n_chunks // 2)
        def _():
            copy(h0 + 2, 0).start()
        copy(h0 + 1, 1).wait()
        compute(h0 + 1, bufs[1])
        return carry

    lax.fori_loop(0, n_chunks // 2, pair, 0)


def _sc_dots(u, idx_tm, tab, t0, n_tok):
    from jax.experimental.pallas import tpu_sc as plsc
    d = u.shape[1]
    n_sel = idx_tm.shape[1]
    per_w = n_tok // SC_WORKERS
    ln, rb_n, ch = SC_LANES, SC_BLOCK, SC_CHUNK
    words = tab.shape[1]

    def body(u_hbm, idx_hbm, tab_hbm, out_hbm, x_v, idx_v, rows0, rows1, d_v, acc_v, sem0, sem1):
        wid = lax.axis_index("s") * SC_CORES + lax.axis_index("c")
        lane = lax.iota(jnp.int32, ln)

        def token(t, carry):
            loc = wid * per_w + t
            pltpu.sync_copy(u_hbm.at[t0 + loc], x_v)
            pltpu.sync_copy(idx_hbm.at[t0 + loc], idx_v)

            def compute(h, rows_v):
                def colblock(cb, accs):
                    base = cb * (rb_n * ln)
                    wbase = cb * (rb_n // 2 * ln)
                    xe = [plsc.load_gather(x_v, [base + k * 2 * ln + 2 * lane]) for k in range(rb_n // 2)]
                    xo = [plsc.load_gather(x_v, [base + k * 2 * ln + 2 * lane + 1]) for k in range(rb_n // 2)]
                    out = []
                    for r in range(ch):
                        acc = accs[r]
                        for k in range(rb_n // 2):
                            wv = rows_v[r, pl.ds(wbase + k * ln, ln)]
                            acc = acc + _bf16_low(wv) * xe[k] + _bf16_high(wv) * xo[k]
                        out.append(acc)
                    return tuple(out)
                accs = lax.fori_loop(0, d // (rb_n * ln), colblock, tuple(jnp.zeros((ln,), F32) for _ in range(ch)))
                for g in range(ch // ln):
                    for r in range(ln):
                        acc_v[r, :] = accs[g * ln + r]
                    dvec = jnp.zeros((ln,), F32)
                    for l in range(ln):
                        dvec = dvec + plsc.load_gather(acc_v, [lane, jnp.full((ln,), l, jnp.int32)])
                    d_v[pl.ds(h * ch + g * ln, ln)] = dvec

            _sc_gather_pipeline(tab_hbm, idx_v, (rows0, rows1), (sem0, sem1), n_sel // ch, compute)
            pltpu.sync_copy(d_v, out_hbm.at[loc])
            return carry

        lax.fori_loop(0, per_w, token, 0)

    return pl.kernel(
        body, out_type=jax.ShapeDtypeStruct((n_tok, n_sel), F32), mesh=_sc_mesh(),
        scratch_types=[pltpu.VMEM((d,), F32), pltpu.VMEM((n_sel,), jnp.int32),
                       pltpu.VMEM((ch, words), jnp.int32), pltpu.VMEM((ch, words), jnp.int32),
                       pltpu.VMEM((n_sel,), F32), pltpu.VMEM((ln, ln), F32),
                       pltpu.SemaphoreType.DMA, pltpu.SemaphoreType.DMA],
        compiler_params=_sc_params(), name="sc_dots")(u, idx_tm, tab)


def _sc_mix(act, idx_tm, tab, t0, n_tok):
    from jax.experimental.pallas import tpu_sc as plsc
    words = tab.shape[1]
    d = 2 * words
    n_sel = idx_tm.shape[1]
    per_w = n_tok // SC_WORKERS
    ln, cb_n, ch = SC_LANES, SC_BLOCK, SC_CHUNK

    def body(act_hbm, idx_hbm, tab_hbm, out_hbm, a_v, idx_v, rows0, rows1, sp_v, y_v, ye_v, yo_v, sem0, sem1):
        wid = lax.axis_index("s") * SC_CORES + lax.axis_index("c")
        lane = lax.iota(jnp.int32, ln)

        def token(t, carry):
            loc = wid * per_w + t
            pltpu.sync_copy(act_hbm.at[loc], a_v)
            pltpu.sync_copy(idx_hbm.at[t0 + loc], idx_v)

            def splat(g, c2):
                avec = a_v[pl.ds(g * ln, ln)]
                for r in range(ln):
                    sp_v[g * ln + r, :] = jnp.full((ln,), jnp.sum(jnp.where(lane == r, avec, 0.0)), F32)
                return c2
            lax.fori_loop(0, n_sel // ln, splat, 0)

            def zero(c, c2):
                ye_v[pl.ds(c * ln, ln)] = jnp.zeros((ln,), F32)
                yo_v[pl.ds(c * ln, ln)] = jnp.zeros((ln,), F32)
                return c2
            lax.fori_loop(0, words // ln, zero, 0)

            def compute(h, rows_v):
                def colblock(cb, c2):
                    wbase = cb * (cb_n // 2 * ln)
                    ev = [ye_v[pl.ds(wbase + k * ln, ln)] for k in range(cb_n // 2)]
                    od = [yo_v[pl.ds(wbase + k * ln, ln)] for k in range(cb_n // 2)]
                    for r in range(ch):
                        ar = sp_v[h * ch + r, :]
                        for k in range(cb_n // 2):
                            wv = rows_v[r, pl.ds(wbase + k * ln, ln)]
                            ev[k] = ev[k] + _bf16_low(wv) * ar
                            od[k] = od[k] + _bf16_high(wv) * ar
                    for k in range(cb_n // 2):
                        ye_v[pl.ds(wbase + k * ln, ln)] = ev[k]
                        yo_v[pl.ds(wbase + k * ln, ln)] = od[k]
                    return c2
                lax.fori_loop(0, d // (cb_n * ln), colblock, 0)

            _sc_gather_pipeline(tab_hbm, idx_v, (rows0, rows1), (sem0, sem1), n_sel // ch, compute)

            def weave(c, c2):
                plsc.store_scatter(y_v, [c * 2 * ln + 2 * lane], ye_v[pl.ds(c * ln, ln)])
                plsc.store_scatter(y_v, [c * 2 * ln + 2 * lane + 1], yo_v[pl.ds(c * ln, ln)])
                return c2
            lax.fori_loop(0, words // ln, weave, 0)
            pltpu.sync_copy(y_v, out_hbm.at[loc])
            return carry

        lax.fori_loop(0, per_w, token, 0)

    return pl.kernel(
        body, out_type=jax.ShapeDtypeStruct((n_tok, d), F32), mesh=_sc_mesh(),
        scratch_types=[pltpu.VMEM((n_sel,), F32), pltpu.VMEM((n_sel,), jnp.int32),
                       pltpu.VMEM((ch, words), jnp.int32), pltpu.VMEM((ch, words), jnp.int32),
                       pltpu.VMEM((n_sel, ln), F32), pltpu.VMEM((d,), F32),
                       pltpu.VMEM((words,), F32), pltpu.VMEM((words,), F32),
                       pltpu.SemaphoreType.DMA, pltpu.SemaphoreType.DMA],
        compiler_params=_sc_params(), name="sc_mix")(act, idx_tm, tab)


def _act_kernel(d_ref, w_ref, after_ref, o_ref):
    del after_ref
    o_ref[...] = _gelu(d_ref[...]) * w_ref[...]


def _activation(dots, w_tm, after):
    n, k = dots.shape
    tm = min(ROW_TILE * 4, n)
    spec = pl.BlockSpec((tm, k), lambda i: (i, 0))
    anchor = pl.BlockSpec((SUBLANES, after.shape[1]), lambda i: (0, 0))
    return pl.pallas_call(_act_kernel, grid=(n // tm,), in_specs=[spec, spec, anchor], out_specs=spec,
                          out_shape=jax.ShapeDtypeStruct((n, k), F32),
                          compiler_params=_cparams(("arbitrary",)), name="peer_act")(dots, w_tm, after)


def _finish_kernel(x_ref, y_ref, mod_ref, fg_ref, o_ref):
    o_ref[...] = _rms(x_ref[...] + mod_ref[0] * y_ref[...], fg_ref[...])


def _finish(x1, y, gate_mod, fg, seq_len, t0, out_prev):
    n, d = y.shape
    tm = min(ROW_TILE, seq_len)
    per_seq = seq_len // tm
    first = t0 // tm
    return pl.pallas_call(
        _drop_operand(_finish_kernel, 4),
        grid=(n // tm,),
        in_specs=[pl.BlockSpec((tm, d), lambda i: (first + i, 0)),
                  pl.BlockSpec((tm, d), lambda i: (i, 0)),
                  pl.BlockSpec((1, 1, d), lambda i: ((first + i) // per_seq, 0, 0)),
                  pl.BlockSpec((1, d), lambda i: (0, 0)),
                  pl.BlockSpec(memory_space=pl.ANY)],
        out_specs=pl.BlockSpec((tm, d), lambda i: (first + i, 0)),
        out_shape=jax.ShapeDtypeStruct(x1.shape, F32),
        input_output_aliases={4: 0},
        compiler_params=_cparams(("arbitrary",)),
        name="peer_finish",
    )(x1, y, gate_mod, fg, out_prev)


def _block_diag(w):
    g, i, j = w.shape
    eye = jnp.eye(g, dtype=w.dtype)
    return (eye[:, None, :, None] * w[:, :, None, :]).reshape(g * i, g * j)


def kernel(x, c, ctx, c_ctx, ada_w, ada_b, norm1_g, w_in, conv_w, conv_b, rg_w_a, rg_b_a, rg_w_x, rg_b_x, rg_lambda, gla_w_g, gla_b_g, gla_norm_g, w_out, norm2_g, peer_w_q, peer_keys, peer_u, peer_v, final_norm_g):
    bsz, t_lat, d = x.shape
    t_ctx = ctx.shape[1]
    assert ada_w.shape[0] == 1, "single-layer block"
    n = bsz * t_lat
    x2 = x.reshape(n, d)
    ctx2 = ctx.reshape(bsz * t_ctx, d)
    pos = _sincos_2d(t_lat, d)

    pad_rows = (-(bsz + 1)) % SUBLANES
    cc = jnp.concatenate([c, c_ctx[None, :], jnp.zeros((pad_rows, d), F32)], axis=0)
    mod = _adaln(cc, ada_w[0], ada_b[0][None, :])
    mod_l = mod[:bsz].reshape(bsz, N_MOD, 1, d)
    mod_c = mod[bsz].reshape(N_MOD, 1, 1, d)

    w_in_p = jnp.pad(w_in[0], ((0, 0), (0, IN_COLS_PAD - w_in.shape[2]))).astype(BF16)
    g1 = norm1_g[0][None, :]
    h_l = _inproj(x2, pos, mod_l[:, 0], mod_l[:, 1], g1, w_in_p, t_lat)
    h_c = _inproj(ctx2, None, mod_c[0], mod_c[1], g1, w_in_p, t_ctx)

    wg_rg = jnp.concatenate([_block_diag(rg_w_a[0, 0]), _block_diag(rg_w_x[0, 0]),
                             _block_diag(rg_w_a[0, 1]), _block_diag(rg_w_x[0, 1])], axis=1).astype(BF16)
    bg_rg = jnp.concatenate([rg_b_a[0, 0], rg_b_x[0, 0], rg_b_a[0, 1], rg_b_x[0, 1]])[None, :]
    y_rg = _rglru(h_l, h_c, conv_w[0], conv_b[0][None, :], wg_rg, bg_rg, rg_lambda[0], bsz, t_lat, t_ctx)

    wg_gla = jnp.zeros((2, LANES, QK_WIDTH), F32)
    wg_gla = wg_gla.at[0, 0:GLA_RANK].set(gla_w_g[0, 0]).at[1, GLA_RANK:2 * GLA_RANK].set(gla_w_g[0, 1])
    o_gla = _gla(h_l, h_c, wg_gla, gla_b_g[0][:, None, :], bsz, t_lat, t_ctx)

    x1 = _merge(y_rg, h_l, o_gla, x2, pos, mod_l[:, 2], gla_norm_g[0][None, :], w_out[0].astype(BF16), t_lat)

    half = PEER_DQ // 2
    wq = peer_w_q[0].reshape(d, PEER_HEADS, 2, half).transpose(0, 2, 1, 3).reshape(d, 2 * PEER_HEADS * half).astype(BF16)
    kt = peer_keys[0].transpose(1, 2, 0, 3)
    wk = (kt[:, :, :, None, :] * jnp.eye(PEER_HEADS, dtype=F32)[None, None, :, :, None])
    wk = wk.reshape(2, N_KEYS * PEER_HEADS, PEER_HEADS * half).astype(BF16)
    u, eidx, wts = _select(x1, mod_l[:, 3], mod_l[:, 4], norm2_g[0][None, :], wq, wk, t_lat)
    tt = GATHER_TOKENS
    idx_tm = eidx.T
    idx3 = idx_tm.reshape(n // tt, 1, tt * N_SEL)
    u_tiles = peer_u[0].reshape(peer_u.shape[1], d // LANES, LANES)
    v_tiles = peer_v[0].reshape(peer_v.shape[1], d // LANES, LANES)
    uv_tab = jnp.concatenate([u_tiles, v_tiles], axis=1).astype(BF16)
    fg = final_norm_g[None, :]
    sc_seqs = SC_SEQUENCES if (bsz > SC_SEQUENCES and t_lat % (4 * ROW_TILE) == 0) else 0
    n_tc = (bsz - sc_seqs) * t_lat
    gate = mod_l[:, 5]
    if sc_seqs:
        n_a = min(TC_FIRST_SEQUENCES, bsz - sc_seqs - 1) * t_lat
        n_sc = n - n_tc
        out_a = _peer(idx3, u, wts, x1, gate, fg, uv_tab, t_lat, 0, n_a)
        dots = _sc_dots(u, idx_tm, _pack_bf16_rows(peer_u[0]), n_tc, n_sc)
        act = _activation(dots, wts[:, n_tc:].T, out_a)
        out_b = _peer(idx3, u, wts, x1, gate, fg, uv_tab, t_lat, n_a, n_tc - n_a, out_a)
        y_sc = _sc_mix(act, idx_tm, _pack_bf16_rows(peer_v[0]), n_tc, n_sc)
        out = _finish(x1, y_sc, gate, fg, t_lat, n_tc, out_b)
    else:
        out = _peer(idx3, u, wts, x1, gate, fg, uv_tab, t_lat, 0, n)
    return out.reshape(bsz, t_lat, d)
```
